```python
import math
import jax, jax.numpy as jnp
from jax import lax
import numpy as np

D_MODEL = 4096
BATCH = 1
SEQ = 8192
DEPTH = 1

CTX_LEN = 256
GRID_W = 64
N_DIFF_HEADS = 16
DIFF_HEAD_DIM = 64
ATTN_WIDTH = N_DIFF_HEADS * 2 * DIFF_HEAD_DIM
POOL_WINDOWS = (2, 4, 8, 16)
N_POOL_GROUPS = len(POOL_WINDOWS)
POOL_WIDTH = D_MODEL // 2
POOL_GROUP_DIM = POOL_WIDTH // N_POOL_GROUPS
N_EXPERTS = 16
EXPERT_FF = D_MODEL // 2
EC_CAPACITY_FACTOR = 2
ROPE_BASE = 10000.0
Q_BLOCK = 128
LN_EPS = 1e-5
ADALN_EPS = 1e-6
SUBLN_EPS = 1e-5
DEEPNORM_ALPHA = (2.0 * DEPTH) ** 0.25
DEEPNORM_BETA = (8.0 * DEPTH) ** -0.25

Q_OFF = 0
K_OFF = Q_OFF + ATTN_WIDTH
V_OFF = K_OFF + ATTN_WIDTH
P_OFF = V_OFF + ATTN_WIDTH
GA_OFF = P_OFF + POOL_WIDTH
GB_OFF = GA_OFF + D_MODEL
IN_WIDTH = GB_OFF + D_MODEL

kernel_name = "hybrid_diffattn_pool_ec_dit_layer"

_F32 = jnp.float32


def _ln(x, eps):
    xf = x.astype(_F32)
    mu = jnp.mean(xf, axis=-1, keepdims=True)
    var = jnp.mean(jnp.square(xf - mu), axis=-1, keepdims=True)
    return (xf - mu) * lax.rsqrt(var + eps)


def _post_ln(x, g, b):
    return (_ln(x, LN_EPS) * g + b).astype(x.dtype)


def _modulate(x, shift, scale):
    return (_ln(x, ADALN_EPS) * (1.0 + scale) + shift).astype(x.dtype)


def _modulation(cond, w_mod, b_mod):
    return jnp.split(jax.nn.silu(cond) @ w_mod + b_mod, 6, axis=-1)


def _axial_rope_tables(n_tokens):
    rows = n_tokens // GRID_W
    row = jnp.broadcast_to(jnp.arange(rows, dtype=_F32)[:, None], (rows, GRID_W)).reshape(-1)
    col = jnp.broadcast_to(jnp.arange(GRID_W, dtype=_F32)[None, :], (rows, GRID_W)).reshape(-1)
    axis_dim = DIFF_HEAD_DIM // 2
    inv_freq = ROPE_BASE ** (-jnp.arange(0, axis_dim, 2, dtype=_F32) / axis_dim)
    ang = jnp.stack([row[:, None] * inv_freq, col[:, None] * inv_freq], axis=1)
    return jnp.cos(ang), jnp.sin(ang)


def _apply_axial_rope(x, cos, sin):
    shp = x.shape
    xf = x.astype(_F32).reshape(*shp[:-1], 2, 2, DIFF_HEAD_DIM // 4)
    x1, x2 = xf[..., 0, :], xf[..., 1, :]
    c = cos[None, :, None, None]
    s = sin[None, :, None, None]
    out = jnp.stack([x1 * c - x2 * s, x2 * c + x1 * s], axis=-2)
    return out.reshape(shp).astype(x.dtype)


def _split_proj(proj):
    B, N, _ = proj.shape
    q = proj[..., Q_OFF:K_OFF].reshape(B, N, N_DIFF_HEADS, 2, DIFF_HEAD_DIM)
    k = proj[..., K_OFF:V_OFF].reshape(B, N, N_DIFF_HEADS, 2, DIFF_HEAD_DIM)
    v = proj[..., V_OFF:P_OFF].reshape(B, N, N_DIFF_HEADS, 2 * DIFF_HEAD_DIM)
    return q, k, v, proj[..., P_OFF:GA_OFF], proj[..., GA_OFF:GB_OFF], proj[..., GB_OFF:]


def _diff_attend(q, k, v, lam):
    s = jnp.einsum('bqhmd,bkhmd->bhmqk', q, k).astype(_F32) * (DIFF_HEAD_DIM ** -0.5)
    a = jax.nn.softmax(s, axis=-1)
    w = (a[:, :, 0] - lam * a[:, :, 1]).astype(v.dtype)
    return jnp.einsum('bhqk,bkhe->bqhe', w, v)


def _latent_diff_attention(q, k_all, v_all, lam):
    B, S = q.shape[:2]
    n_blk = S // Q_BLOCK
    qb = jnp.moveaxis(q.reshape(B, n_blk, Q_BLOCK, N_DIFF_HEADS, 2, DIFF_HEAD_DIM), 1, 0)
    o = lax.map(lambda q_blk: _diff_attend(q_blk, k_all, v_all, lam), qb)
    return jnp.moveaxis(o, 0, 1).reshape(B, S, N_DIFF_HEADS, 2 * DIFF_HEAD_DIM)


def _attn_post(o, subln_g, lam_init):
    B, N = o.shape[:2]
    of = o.astype(_F32)
    of = of * lax.rsqrt(jnp.mean(jnp.square(of), axis=-1, keepdims=True) + SUBLN_EPS)
    of = of * subln_g * (1.0 - lam_init)
    return of.reshape(B, N, ATTN_WIDTH).astype(o.dtype)


def _multiscale_pool(p, w_pool, pool_scale):
    B, N, _ = p.shape
    pf = p.astype(_F32)
    t = jnp.arange(N)
    outs = []
    for g, w in enumerate(POOL_WINDOWS):
        pg = pf[..., g * POOL_GROUP_DIM:(g + 1) * POOL_GROUP_DIM]
        csum = jnp.concatenate([jnp.zeros((B, 1, POOL_GROUP_DIM), _F32), jnp.cumsum(pg, axis=1)], axis=1)
        lo = jnp.clip(t - w // 2, 0, N)
        hi = jnp.clip(t + w // 2, 0, N)
        mean = (csum[:, hi] - csum[:, lo]) / (hi - lo).astype(_F32)[None, :, None]
        outs.append(mean - pg)
    m = jnp.stack(outs, axis=2).astype(p.dtype)
    y = jnp.einsum('bngc,gce->bnge', m, w_pool).reshape(B, N, POOL_WIDTH)
    return y * pool_scale


def _mixer_merge(attn_flat, p, ga, gb, w_pool, pool_scale, w_br_a, w_br_b, w_out):
    pooled = _multiscale_pool(p, w_pool, pool_scale)
    merged = jax.nn.sigmoid(ga) * (attn_flat @ w_br_a) + jax.nn.sigmoid(gb) * (pooled @ w_br_b)
    return merged @ w_out


def _expert_choice_ffn(h, w_router, w_gate, w_up, w_down):
    B, N, D = h.shape
    cap = EC_CAPACITY_FACTOR * N // N_EXPERTS
    aff = jax.nn.softmax((h @ w_router).astype(_F32), axis=-1)
    g, idx = lax.top_k(jnp.swapaxes(aff, 1, 2), cap)
    h_sel = jax.vmap(lambda hb, ib: hb[ib])(h, idx)
    u = jnp.einsum('becd,edf->becf', h_sel, w_gate)
    v = jnp.einsum('becd,edf->becf', h_sel, w_up)
    o = jnp.einsum('becf,efd->becd', jax.nn.silu(u) * v, w_down) * g[..., None].astype(h.dtype)
    return jax.vmap(lambda ob, ib: jnp.zeros((N, D), ob.dtype).at[ib.reshape(-1)].add(ob.reshape(-1, D)))(o, idx)


def setup_inputs(seed: int = 0) -> dict:
    key = jax.random.key(seed)
    ks = jax.random.split(key, 24)
    D = D_MODEL

    def nrm(k, shape, scale):
        return jax.random.normal(k, shape, _F32) * scale

    beta = DEEPNORM_BETA
    w_in = jnp.concatenate([
        nrm(ks[6], (DEPTH, D, 2 * ATTN_WIDTH), D ** -0.5),
        nrm(ks[7], (DEPTH, D, ATTN_WIDTH), beta * D ** -0.5),
        nrm(ks[8], (DEPTH, D, POOL_WIDTH + 2 * D), D ** -0.5),
    ], axis=-1)
    return {
        "x": nrm(ks[0], (BATCH, SEQ, D), 1.0),
        "c": nrm(ks[1], (BATCH, D), 1.0),
        "ctx": nrm(ks[2], (BATCH, CTX_LEN, D), 1.0),
        "c_ctx": nrm(ks[3], (D,), 1.0),
        "w_mod": nrm(ks[4], (DEPTH, D, 6 * D), 0.5 * D ** -0.5),
        "b_mod": nrm(ks[5], (DEPTH, 6 * D), 0.02),
        "w_in": w_in,
        "diff_lambda": nrm(ks[9], (DEPTH, 4, DIFF_HEAD_DIM), 0.1),
        "diff_subln_g": 1.0 + nrm(ks[10], (DEPTH, 2 * DIFF_HEAD_DIM), 0.05),
        "w_pool": nrm(ks[11], (DEPTH, N_POOL_GROUPS, POOL_GROUP_DIM, POOL_GROUP_DIM), beta * POOL_GROUP_DIM ** -0.5),
        "pool_scale": 1.0 + nrm(ks[12], (DEPTH, POOL_WIDTH), 0.1),
        "w_branch_attn": nrm(ks[13], (DEPTH, ATTN_WIDTH, D), beta * ATTN_WIDTH ** -0.5),
        "w_branch_pool": nrm(ks[14], (DEPTH, POOL_WIDTH, D), beta * POOL_WIDTH ** -0.5),
        "w_out": nrm(ks[15], (DEPTH, D, D), beta * D ** -0.5),
        "ln1_g": 1.0 + nrm(ks[16], (DEPTH, D), 0.05),
        "ln1_b": nrm(ks[17], (DEPTH, D), 0.02),
        "w_router": nrm(ks[18], (DEPTH, D, N_EXPERTS), D ** -0.5),
        "w_gate": nrm(ks[19], (DEPTH, N_EXPERTS, D, EXPERT_FF), beta * D ** -0.5),
        "w_up": nrm(ks[20], (DEPTH, N_EXPERTS, D, EXPERT_FF), beta * D ** -0.5),
        "w_down": nrm(ks[21], (DEPTH, N_EXPERTS, EXPERT_FF, D), beta * EXPERT_FF ** -0.5),
        "ln2_g": 1.0 + nrm(ks[22], (DEPTH, D), 0.05),
        "ln2_b": nrm(ks[23], (DEPTH, D), 0.02),
    }


def reference(x, c, ctx, c_ctx, w_mod, b_mod, w_in, diff_lambda, diff_subln_g, w_pool, pool_scale,
              w_branch_attn, w_branch_pool, w_out, ln1_g, ln1_b, w_router, w_gate, w_up, w_down,
              ln2_g, ln2_b):
    S = x.shape[1]
    cos, sin = _axial_rope_tables(S)
    alpha = DEEPNORM_ALPHA
    for l in range(DEPTH):
        last = l == DEPTH - 1
        lam_init = 0.8 - 0.6 * math.exp(-0.3 * l)
        lq1, lk1, lq2, lk2 = diff_lambda[l]
        lam = (jnp.exp(jnp.sum((lq1 * lk1).astype(_F32))) - jnp.exp(jnp.sum((lq2 * lk2).astype(_F32)))
               + lam_init)
        sh1, sc1, g1, sh2, sc2, g2 = [m[:, None, :] for m in _modulation(c, w_mod[l], b_mod[l])]
        csh1, csc1, cg1, csh2, csc2, cg2 = _modulation(c_ctx, w_mod[l], b_mod[l])
        mix_w = (w_pool[l], pool_scale[l], w_branch_attn[l], w_branch_pool[l], w_out[l])

        q, k, v, p, ga, gb = _split_proj(_modulate(x, sh1, sc1) @ w_in[l])
        qc, kc, vc, pc, gac, gbc = _split_proj(_modulate(ctx, csh1, csc1) @ w_in[l])
        q = _apply_axial_rope(q, cos, sin)
        k = _apply_axial_rope(k, cos, sin)
        k_all = jnp.concatenate([k, kc], axis=1)
        v_all = jnp.concatenate([v, vc], axis=1)
        o = _latent_diff_attention(q, k_all, v_all, lam)
        y = _mixer_merge(_attn_post(o, diff_subln_g[l], lam_init), p, ga, gb, *mix_w)
        x = _post_ln(alpha * x + g1 * y, ln1_g[l], ln1_b[l])
        if not last:
            oc = _diff_attend(qc, kc, vc, lam)
            yc = _mixer_merge(_attn_post(oc, diff_subln_g[l], lam_init), pc, gac, gbc, *mix_w)
            ctx = _post_ln(alpha * ctx + cg1 * yc, ln1_g[l], ln1_b[l])

        ffn_w = (w_router[l], w_gate[l], w_up[l], w_down[l])
        y2 = _expert_choice_ffn(_modulate(x, sh2, sc2), *ffn_w)
        x = _post_ln(alpha * x + g2 * y2, ln2_g[l], ln2_b[l])
        if not last:
            yc2 = _expert_choice_ffn(_modulate(ctx, csh2, csc2), *ffn_w)
            ctx = _post_ln(alpha * ctx + cg2 * yc2, ln2_g[l], ln2_b[l])
    return x
```

```python
import functools
import math

import jax
import jax.numpy as jnp
from jax import lax
from jax.experimental import pallas as pl
from jax.experimental.pallas import tpu as pltpu

F32 = jnp.float32
BF16 = jnp.bfloat16
I32 = jnp.int32
U32 = jnp.uint32

LANES = 128
SUBLANES = 8
VMEM_LIMIT = 56 * 1024 * 1024

GRID_W = 64
N_HEADS = 16
HEAD_DIM = 64
V_DIM = 2 * HEAD_DIM
POOL_WINDOWS = (2, 4, 8, 16)
POOL_HALO = 8
EC_CAPACITY_FACTOR = 2
ROPE_BASE = 10000.0
LN_EPS = 1e-5
ADALN_EPS = 1e-6
SUBLN_EPS = 1e-5
DEPTH = 1
DEEPNORM_ALPHA = (2.0 * DEPTH) ** 0.25
LAMBDA_INIT = 0.8 - 0.6 * math.exp(-0.3 * 0)
NEG_BIG = -1e30
BISECT_STEPS = 64
GATHER_UNROLL = 8


def _pick(n, prefs):
    for p in prefs:
        if n % p == 0:
            return p
    raise ValueError(f"no tile for {n} in {prefs}")


def _params(n_axes):
    return pltpu.CompilerParams(dimension_semantics=("arbitrary",) * n_axes,
                                vmem_limit_bytes=VMEM_LIMIT)


def _ln_rows(x, eps):
    mu = jnp.mean(x, axis=-1, keepdims=True)
    xc = x - mu
    var = jnp.mean(xc * xc, axis=-1, keepdims=True)
    return xc * lax.rsqrt(var + eps)


def _mod_kernel(cb_ref, w_ref, b_ref, o_ref, *, tn):
    o_ref[...] = jnp.zeros_like(o_ref)
    for i in range(2):
        cv = cb_ref[i]
        s = cv * jax.nn.sigmoid(cv)
        for j in range(tn // LANES):
            cols = slice(j * LANES, (j + 1) * LANES)
            o_ref[i:i + 1, cols] = jnp.sum(w_ref[:, cols] * s, axis=0, keepdims=True) + b_ref[:, cols]


def _modulation(c, c_ctx, w_mod, b_mod):
    d, n = w_mod.shape
    tn = _pick(n, (512, 256, 128))
    cb = jnp.broadcast_to(jnp.stack([c[0], c_ctx])[:, :, None], (2, d, LANES))
    return pl.pallas_call(
        functools.partial(_mod_kernel, tn=tn),
        grid=(n // tn,),
        in_specs=[pl.BlockSpec((2, d, LANES), lambda j: (0, 0, 0)),
                  pl.BlockSpec((d, tn), lambda j: (0, j)),
                  pl.BlockSpec((1, tn), lambda j: (0, j))],
        out_specs=pl.BlockSpec((SUBLANES, tn), lambda j: (0, j)),
        out_shape=jax.ShapeDtypeStruct((SUBLANES, n), F32),
        compiler_params=_params(1),
        name="modulation",
    )(cb, w_mod, b_mod.reshape(1, n))


def _premod_kernel(x_ref, sh_ref, sc_ref, o_ref, *, row):
    h = _ln_rows(x_ref[...], ADALN_EPS) * (1.0 + sc_ref[row:row + 1, :]) + sh_ref[row:row + 1, :]
    o_ref[...] = h.astype(BF16)


def _premod(x2d, mods, row):
    m, d = x2d.shape
    tm = _pick(m, (256, 128))
    return pl.pallas_call(
        functools.partial(_premod_kernel, row=row),
        grid=(m // tm,),
        in_specs=[pl.BlockSpec((tm, d), lambda i: (i, 0)),
                  pl.BlockSpec((SUBLANES, d), lambda i: (0, 0)),
                  pl.BlockSpec((SUBLANES, d), lambda i: (0, 1))],
        out_specs=pl.BlockSpec((tm, d), lambda i: (i, 0)),
        out_shape=jax.ShapeDtypeStruct((m, d), BF16),
        compiler_params=_params(1),
        name="premod",
    )(x2d, mods, mods)


def _proj_kernel(*refs, rope, scale, tn):
    if rope:
        h_ref, w_ref, cos_ref, sa_ref, sb_ref, o_ref, wbf = refs
    else:
        h_ref, w_ref, o_ref, wbf = refs

    @pl.when(pl.program_id(1) == 0)
    def _():
        wbf[...] = w_ref[...].astype(BF16)

    acc = jnp.dot(h_ref[...], wbf[...], preferred_element_type=F32)
    if rope:
        cos, sa, sb = cos_ref[...], sa_ref[...], sb_ref[...]
        for c in range(tn // LANES):
            cols = slice(c * LANES, (c + 1) * LANES)
            a = acc[:, cols]
            r = a * cos + pltpu.roll(a, LANES - 16, 1) * sa + pltpu.roll(a, 16, 1) * sb
            o_ref[:, cols] = (r * scale).astype(o_ref.dtype)
    else:
        o_ref[...] = acc.astype(o_ref.dtype)


def _project(h, w, col_off, n_cols, out_dtype, rope_tabs=None, scale=1.0):
    m, k = h.shape
    tn = _pick(math.gcd(n_cols, col_off), (512, 256, 128))
    tm = _pick(m, (1024, 512, 256, 128))
    off = col_off // tn
    in_specs = [pl.BlockSpec((tm, k), lambda j, i: (i, 0)),
                pl.BlockSpec((k, tn), lambda j, i: (0, j + off))]
    args = [h, w]
    if rope_tabs is not None:
        in_specs += [pl.BlockSpec((tm, LANES), lambda j, i: (i, 0))] * 3
        args += list(rope_tabs)
    return pl.pallas_call(
        functools.partial(_proj_kernel, rope=rope_tabs is not None, scale=scale, tn=tn),
        grid=(n_cols // tn, m // tm),
        in_specs=in_specs,
        out_specs=pl.BlockSpec((tm, tn), lambda j, i: (i, j)),
        out_shape=jax.ShapeDtypeStruct((m, n_cols), out_dtype),
        scratch_shapes=[pltpu.VMEM((k, tn), BF16)],
        compiler_params=_params(2),
        name="project",
    )(*args)


def _rope_tables(n_tokens):
    rows = n_tokens // GRID_W
    row = jnp.broadcast_to(jnp.arange(rows, dtype=F32)[:, None], (rows, GRID_W)).reshape(-1)
    col = jnp.broadcast_to(jnp.arange(GRID_W, dtype=F32)[None, :], (rows, GRID_W)).reshape(-1)
    axis_dim = HEAD_DIM // 2
    nf = axis_dim // 2
    inv_freq = ROPE_BASE ** (-jnp.arange(0, axis_dim, 2, dtype=F32) / axis_dim)
    ang = jnp.stack([row[:, None] * inv_freq, col[:, None] * inv_freq], axis=1)
    cos, sin = jnp.cos(ang), jnp.sin(ang)
    zero = jnp.zeros_like(sin)
    cos64 = jnp.concatenate([cos, cos], axis=-1).reshape(n_tokens, HEAD_DIM)
    sa64 = jnp.concatenate([-sin, zero], axis=-1).reshape(n_tokens, HEAD_DIM)
    sb64 = jnp.concatenate([zero, sin], axis=-1).reshape(n_tokens, HEAD_DIM)
    rep = LANES // HEAD_DIM
    return tuple(jnp.tile(t, (1, rep)) for t in (cos64, sa64, sb64))


def _attn_kernel(q_ref, k_ref, v_ref, kc_ref, vc_ref, dl_ref, g_ref, o_ref, qs_sc, sa_sc, sb_sc, sc_sc,
                 xa_sc, xb_sc, xc_sc, m_sc, l_sc, acc_sc, *, tq, tk, n_chunks):
    q = q_ref[...]
    lane = lax.broadcasted_iota(I32, q.shape, 1)
    zero = jnp.zeros_like(q)
    qs_sc[0:tq, :] = jnp.where(lane < HEAD_DIM, q, zero)
    qs_sc[tq:2 * tq, :] = jnp.where(lane >= HEAD_DIM, q, zero)
    m_sc[...] = jnp.full_like(m_sc, NEG_BIG)
    l_sc[...] = jnp.zeros_like(l_sc)
    acc_sc[...] = jnp.zeros_like(acc_sc)

    def scores(s_ref, x_ref, kb):
        s = lax.dot_general(qs_sc[...], kb, (((1,), (1,)), ((), ())), preferred_element_type=F32)
        s_ref[...] = s
        x_ref[...] = jnp.broadcast_to(jnp.max(s, axis=-1, keepdims=True), x_ref.shape)

    def absorb(s_ref, x_ref, vb):
        m_prev = m_sc[...]
        m_new = jnp.maximum(m_prev, x_ref[...])
        alpha = jnp.exp2(m_prev - m_new)
        p = jnp.exp2(s_ref[...] - jnp.tile(m_new, (1, s_ref.shape[1] // LANES)))
        l_sc[...] = alpha * l_sc[...] + jnp.broadcast_to(jnp.sum(p, axis=-1, keepdims=True), l_sc.shape)
        acc_sc[...] = alpha * acc_sc[...] + jnp.dot(p.astype(BF16), vb, preferred_element_type=F32)
        m_sc[...] = m_new

    def kchunk(c):
        return k_ref[pl.ds(pl.multiple_of(c * tk, tk), tk), :]

    def vchunk(c):
        return v_ref[pl.ds(pl.multiple_of(c * tk, tk), tk), :]

    scores(sa_sc, xa_sc, kchunk(0))

    def body(j, carry):
        c = 2 * j
        scores(sb_sc, xb_sc, kchunk(c + 1))
        absorb(sa_sc, xa_sc, vchunk(c))
        scores(sa_sc, xa_sc, kchunk(c + 2))
        absorb(sb_sc, xb_sc, vchunk(c + 1))
        return carry

    lax.fori_loop(0, (n_chunks - 2) // 2, body, 0)
    scores(sb_sc, xb_sc, kchunk(n_chunks - 1))
    absorb(sa_sc, xa_sc, vchunk(n_chunks - 2))
    scores(sc_sc, xc_sc, kc_ref[...])
    absorb(sb_sc, xb_sc, vchunk(n_chunks - 1))
    absorb(sc_sc, xc_sc, vc_ref[...])

    dl = dl_ref[...]
    lam = (jnp.exp(jnp.sum(dl[0:1] * dl[1:2], axis=-1, keepdims=True))
           - jnp.exp(jnp.sum(dl[2:3] * dl[3:4], axis=-1, keepdims=True)) + LAMBDA_INIT)
    o = acc_sc[0:tq, :] / l_sc[0:tq, :] - lam * (acc_sc[tq:2 * tq, :] / l_sc[tq:2 * tq, :])
    o = o * lax.rsqrt(jnp.mean(o * o, axis=-1, keepdims=True) + SUBLN_EPS)
    o_ref[...] = (o * g_ref[...] * (1.0 - LAMBDA_INIT)).astype(o_ref.dtype)


def _attention(q, k, v, kc, vc, diff_lambda, subln_g):
    s_len, width = q.shape
    c_len = kc.shape[0]
    tq = _pick(s_len, (512, 256, 128))
    tk = _pick(s_len // 2, (1024, 512, 256, 128))
    n_chunks = s_len // tk
    assert n_chunks % 2 == 0
    n_heads = width // V_DIM
    return pl.pallas_call(
        functools.partial(_attn_kernel, tq=tq, tk=tk, n_chunks=n_chunks),
        grid=(n_heads, s_len // tq),
        in_specs=[pl.BlockSpec((tq, V_DIM), lambda h, i: (i, h)),
                  pl.BlockSpec((s_len, V_DIM), lambda h, i: (0, h)),
                  pl.BlockSpec((s_len, V_DIM), lambda h, i: (0, h)),
                  pl.BlockSpec((c_len, V_DIM), lambda h, i: (0, h)),
                  pl.BlockSpec((c_len, V_DIM), lambda h, i: (0, h)),
                  pl.BlockSpec((4, HEAD_DIM), lambda h, i: (0, 0)),
                  pl.BlockSpec((1, V_DIM), lambda h, i: (0, 0))],
        out_specs=pl.BlockSpec((tq, V_DIM), lambda h, i: (i, h)),
        out_shape=jax.ShapeDtypeStruct((s_len, width), BF16),
        scratch_shapes=[pltpu.VMEM((2 * tq, V_DIM), BF16),
                        pltpu.VMEM((2 * tq, tk), F32), pltpu.VMEM((2 * tq, tk), F32),
                        pltpu.VMEM((2 * tq, c_len), F32)]
                       + [pltpu.VMEM((2 * tq, LANES), F32)] * 5
                       + [pltpu.VMEM((2 * tq, V_DIM), F32)],
        compiler_params=_params(2),
        name="diff_attention",
    )(q, k, v, kc, vc, diff_lambda, subln_g.reshape(1, V_DIM))


def _pool_kernel(p_ref, prev_ref, next_ref, w_ref, sc_ref, o_ref, ext, *, tm, n_tokens, gdim):
    i = pl.program_id(0)
    last = pl.num_programs(0) - 1
    ext[0:POOL_HALO, :] = jnp.where(i > 0, prev_ref[...], 0.0)
    ext[POOL_HALO:POOL_HALO + tm, :] = p_ref[...]
    ext[POOL_HALO + tm:2 * POOL_HALO + tm, :] = jnp.where(i < last, next_ref[...], 0.0)
    t = i * tm + lax.broadcasted_iota(I32, (tm, 1), 0)
    for g, w in enumerate(POOL_WINDOWS):
        cols = slice(g * gdim, (g + 1) * gdim)
        ssum = None
        for d in range(-(w // 2), w // 2):
            part = ext[POOL_HALO + d:POOL_HALO + d + tm, cols]
            ssum = part if ssum is None else ssum + part
        cnt = jnp.minimum(t + w // 2, n_tokens) - jnp.maximum(t - w // 2, 0)
        mean = ssum / cnt.astype(F32)
        mg = (mean - p_ref[:, cols]).astype(BF16)
        y = jnp.dot(mg, w_ref[g].astype(BF16), preferred_element_type=F32)
        o_ref[:, cols] = (y * sc_ref[:, cols]).astype(o_ref.dtype)


def _pool(p, w_pool, pool_scale):
    s_len, width = p.shape
    n_groups, gdim, _ = w_pool.shape
    assert n_groups == len(POOL_WINDOWS) and max(POOL_WINDOWS) // 2 <= POOL_HALO
    tm = _pick(s_len, (512, 256, 128))
    hb = tm // POOL_HALO
    n_halo_blocks = s_len // POOL_HALO
    return pl.pallas_call(
        functools.partial(_pool_kernel, tm=tm, n_tokens=s_len, gdim=gdim),
        grid=(s_len // tm,),
        in_specs=[pl.BlockSpec((tm, width), lambda i: (i, 0)),
                  pl.BlockSpec((POOL_HALO, width), lambda i: (jnp.maximum(i * hb - 1, 0), 0)),
                  pl.BlockSpec((POOL_HALO, width), lambda i: (jnp.minimum((i + 1) * hb, n_halo_blocks - 1), 0)),
                  pl.BlockSpec((n_groups, gdim, gdim), lambda i: (0, 0, 0)),
                  pl.BlockSpec((1, width), lambda i: (0, 0))],
        out_specs=pl.BlockSpec((tm, width), lambda i: (i, 0)),
        out_shape=jax.ShapeDtypeStruct((s_len, width), BF16),
        scratch_shapes=[pltpu.VMEM((tm + 2 * POOL_HALO, width), F32)],
        compiler_params=_params(1),
        name="pool_mixer",
    )(p, p, p, w_pool, pool_scale.reshape(1, width))


def _merge_kernel(a_ref, p_ref, wa_ref, wb_ref, ga_ref, gb_ref, o_ref, wabf, wbbf):
    @pl.when(pl.program_id(1) == 0)
    def _():
        wabf[...] = wa_ref[...].astype(BF16)
        wbbf[...] = wb_ref[...].astype(BF16)

    ya = jnp.dot(a_ref[...], wabf[...], preferred_element_type=F32)
    yb = jnp.dot(p_ref[...], wbbf[...], preferred_element_type=F32)
    o_ref[...] = (jax.nn.sigmoid(ga_ref[...]) * ya + jax.nn.sigmoid(gb_ref[...]) * yb).astype(o_ref.dtype)


def _merge(attn, pooled, w_a, w_b, gates):
    m, ka = attn.shape
    kb = pooled.shape[1]
    n = w_a.shape[1]
    tn = _pick(n, (512, 256, 128))
    tm = _pick(m, (1024, 512, 256, 128))
    gb_off = n // tn
    return pl.pallas_call(
        _merge_kernel,
        grid=(n // tn, m // tm),
        in_specs=[pl.BlockSpec((tm, ka), lambda j, i: (i, 0)),
                  pl.BlockSpec((tm, kb), lambda j, i: (i, 0)),
                  pl.BlockSpec((ka, tn), lambda j, i: (0, j)),
                  pl.BlockSpec((kb, tn), lambda j, i: (0, j)),
                  pl.BlockSpec((tm, tn), lambda j, i: (i, j)),
                  pl.BlockSpec((tm, tn), lambda j, i: (i, j + gb_off))],
        out_specs=pl.BlockSpec((tm, tn), lambda j, i: (i, j)),
        out_shape=jax.ShapeDtypeStruct((m, n), BF16),
        scratch_shapes=[pltpu.VMEM((ka, tn), BF16), pltpu.VMEM((kb, tn), BF16)],
        compiler_params=_params(2),
        name="gated_merge",
    )(attn, pooled, w_a, w_b, gates, gates)


def _post1_kernel(x_ref, y_ref, g1_ref, sh_ref, sc_ref, lg_ref, lb_ref, wr_ref, x1_ref, hp_ref, aff_ref, *, half):
    z = DEEPNORM_ALPHA * x_ref[...] + g1_ref[0:1, :] * y_ref[...]
    x1 = _ln_rows(z, LN_EPS) * lg_ref[...] + lb_ref[...]
    x1_ref[...] = x1
    h2 = _ln_rows(x1, ADALN_EPS) * (1.0 + sc_ref[0:1, :]) + sh_ref[0:1, :]
    hb = lax.bitcast_convert_type(h2.astype(BF16).astype(F32), U32)
    hp_ref[...] = (hb[:, :half] >> 16) | (hb[:, half:] & jnp.uint32(0xFFFF0000))
    logits = lax.dot_general(wr_ref[...], h2, (((1,), (1,)), ((), ())),
                             precision=lax.Precision.HIGHEST, preferred_element_type=F32)
    mx = jnp.max(logits, axis=0, keepdims=True)
    ex = jnp.exp(logits - mx)
    aff_ref[...] = ex / jnp.sum(ex, axis=0, keepdims=True)


def _post1(x2d, y, mods, ln_g, ln_b, w_router_t):
    m, d = x2d.shape
    n_exp = w_router_t.shape[0]
    tm = _pick(m, (256, 128))
    row = lambda i: (i, 0)
    vec = lambda i: (0, 0)
    return pl.pallas_call(
        functools.partial(_post1_kernel, half=d // 2),
        grid=(m // tm,),
        in_specs=[pl.BlockSpec((tm, d), row), pl.BlockSpec((tm, d), row),
                  pl.BlockSpec((SUBLANES, d), lambda i: (0, 2)),
                  pl.BlockSpec((SUBLANES, d), lambda i: (0, 3)),
                  pl.BlockSpec((SUBLANES, d), lambda i: (0, 4)),
                  pl.BlockSpec((1, d), vec), pl.BlockSpec((1, d), vec),
                  pl.BlockSpec((n_exp, d), vec)],
        out_specs=[pl.BlockSpec((tm, d), row), pl.BlockSpec((tm, d // 2), row),
                   pl.BlockSpec((n_exp, tm), lambda i: (0, i))],
        out_shape=[jax.ShapeDtypeStruct((m, d), F32), jax.ShapeDtypeStruct((m, d // 2), U32),
                   jax.ShapeDtypeStruct((n_exp, m), F32)],
        compiler_params=_params(1),
        name="ln1_router",
    )(x2d, y, mods, mods, mods, ln_g.reshape(1, d), ln_b.reshape(1, d), w_router_t)


def _route_kernel(aff_ref, idx_ref, csel_sc, *, n_exp, nc, cap, sblk):
    rows = n_exp * nc
    aff3 = aff_ref[...]
    ri = lax.broadcasted_iota(I32, (rows, rows), 0)
    rj = lax.broadcasted_iota(I32, (rows, rows), 1)
    before = jnp.where(((ri // nc) == (rj // nc)) & (rj < ri), 1.0, 0.0).astype(BF16)
    li = lax.broadcasted_iota(I32, (LANES, LANES), 0)
    lj = lax.broadcasted_iota(I32, (LANES, LANES), 1)
    tri = jnp.where(li <= lj, 1.0, 0.0).astype(BF16)
    ones = jnp.ones((LANES, LANES), BF16)

    def per_expert_sum(x3):
        return jnp.sum(jnp.sum(x3, axis=1, keepdims=True), axis=2, keepdims=True)

    def count(mask3):
        return per_expert_sum(jnp.where(mask3, 1.0, 0.0))

    def prefix(mask2):
        mb = jnp.where(mask2, 1.0, 0.0).astype(BF16)
        within = jnp.dot(mb, tri, preferred_element_type=F32)
        row_tot = jnp.dot(mb, ones, preferred_element_type=F32).astype(BF16)
        return within + jnp.dot(before, row_tot, preferred_element_type=F32)

    def rows_of(x3):
        return jnp.broadcast_to(x3, (n_exp, nc, LANES)).reshape(rows, LANES)

    top = jnp.max(jnp.max(aff3, axis=1, keepdims=True), axis=2, keepdims=True)
    lo0 = jnp.zeros_like(top)
    hi0 = 2.0 * top + 1e-30

    def bisect(_, carry):
        lo, hi = carry
        mid = 0.5 * (lo + hi)
        ok = count(aff3 >= mid) >= cap
        return jnp.where(ok, mid, lo), jnp.where(ok, hi, mid)

    thr3, _ = lax.fori_loop(0, BISECT_STEPS, bisect, (lo0, hi0))
    need = rows_of(cap - count(aff3 > thr3))
    aff2 = aff3.reshape(rows, LANES)
    thr2 = rows_of(thr3)
    gt = aff2 > thr2
    eq = aff2 == thr2
    sel = gt | (eq & (prefix(eq) <= need))
    csel_sc[...] = jnp.where(sel, prefix(sel), 0.0)

    lane_i = lax.broadcasted_iota(I32, (1, LANES), 1)
    slot0 = lax.broadcasted_iota(I32, (sblk, LANES), 0).astype(F32) + 1.0

    def per_expert(e, carry):
        for sb in range(cap // sblk):
            slot = slot0 + float(sb * sblk)

            def per_chunk(c, acc):
                hit = csel_sc[pl.ds(e * nc + c, 1), :] == slot
                return acc + jnp.where(hit, (lane_i + c * LANES).astype(F32), 0.0)

            acc = lax.fori_loop(0, nc, per_chunk, jnp.zeros((sblk, LANES), F32))
            idx_ref[e, sb * sblk:(sb + 1) * sblk, :] = jnp.sum(acc, axis=-1, keepdims=True).astype(I32)
        return carry

    lax.fori_loop(0, n_exp, per_expert, 0)


def _route(aff_t, cap):
    n_exp, s_len = aff_t.shape
    nc = s_len // LANES
    rows = n_exp * nc
    sblk = _pick(cap, (256, 128))
    return pl.pallas_call(
        functools.partial(_route_kernel, n_exp=n_exp, nc=nc, cap=cap, sblk=sblk),
        grid=(1,),
        in_specs=[pl.BlockSpec((n_exp, nc, LANES), lambda i: (0, 0, 0))],
        out_specs=pl.BlockSpec((n_exp, cap, 1), lambda i: (0, 0, 0)),
        out_shape=jax.ShapeDtypeStruct((n_exp, cap, 1), I32),
        scratch_shapes=[pltpu.VMEM((rows, LANES), F32)],
        compiler_params=_params(1),
        name="expert_choice",
    )(aff_t.reshape(n_exp, nc, LANES))


def _ffn_kernel(idx_ref, hp_hbm, wg_ref, wu_ref, wd_ref, o_ref, hsel, hb, act, sem,
                *, n_exp, cap, nf, tf, half):
    e = pl.program_id(0)
    s = pl.program_id(1)

    def start_gather(ee):
        def go(r, carry):
            t = idx_ref[ee * cap + r]
            pltpu.make_async_copy(hp_hbm.at[pl.ds(t, 1), :], hsel.at[pl.ds(r, 1), :], sem.at[0]).start()
            return carry
        lax.fori_loop(0, cap, go, 0, unroll=GATHER_UNROLL)

    def wait_gather():
        pltpu.make_async_copy(hp_hbm.at[pl.ds(0, cap), :], hsel, sem.at[0]).wait()

    @pl.when((e == 0) & (s == 0))
    def _():
        start_gather(0)

    @pl.when(s == 0)
    def _():
        wait_gather()
        u = hsel[...]
        hb[:, :half] = lax.bitcast_convert_type(u << 16, F32).astype(BF16)
        hb[:, half:] = lax.bitcast_convert_type(u & jnp.uint32(0xFFFF0000), F32).astype(BF16)

        @pl.when(e + 1 < n_exp)
        def _():
            start_gather(e + 1)

    @pl.when(s < nf)
    def _():
        x = hb[...]
        u = jnp.dot(x, wg_ref[0].astype(BF16), preferred_element_type=F32)
        v = jnp.dot(x, wu_ref[0].astype(BF16), preferred_element_type=F32)
        act[s] = (u * jax.nn.sigmoid(u) * v).astype(BF16)

    @pl.when(s >= nf)
    def _():
        o = jnp.dot(act[0], wd_ref[0, 0:tf, :].astype(BF16), preferred_element_type=F32)
        for f in range(1, nf):
            o += jnp.dot(act[f], wd_ref[0, f * tf:(f + 1) * tf, :].astype(BF16), preferred_element_type=F32)
        o_ref[...] = o.astype(o_ref.dtype)


def _expert_ffn(idx_flat, hp, w_gate, w_up, w_down):
    n_exp, d, ff = w_gate.shape
    cap = idx_flat.shape[0] // n_exp
    half = hp.shape[1]
    tf = _pick(ff, (256, 128))
    tn = _pick(d, (512, 256, 128))
    nf, nn = ff // tf, d // tn
    last_e = n_exp - 1

    def up_map(e, s, idx):
        return (jnp.where(s < nf, e, jnp.minimum(e + 1, last_e)), 0, jnp.where(s < nf, s, 0))

    def down_map(e, s, idx):
        return (e, 0, jnp.maximum(s - nf, 0))

    def out_map(e, s, idx):
        return (e, jnp.maximum(s - nf, 0))

    grid_spec = pltpu.PrefetchScalarGridSpec(
        num_scalar_prefetch=1,
        grid=(n_exp, nf + nn),
        in_specs=[pl.BlockSpec(memory_space=pl.ANY),
                  pl.BlockSpec((1, d, tf), up_map),
                  pl.BlockSpec((1, d, tf), up_map),
                  pl.BlockSpec((1, ff, tn), down_map)],
        out_specs=pl.BlockSpec((cap, tn), out_map),
        scratch_shapes=[pltpu.VMEM((cap, half), U32), pltpu.VMEM((cap, d), BF16),
                        pltpu.VMEM((nf, cap, tf), BF16), pltpu.SemaphoreType.DMA((1,))],
    )
    return pl.pallas_call(
        functools.partial(_ffn_kernel, n_exp=n_exp, cap=cap, nf=nf, tf=tf, half=half),
        grid_spec=grid_spec,
        out_shape=jax.ShapeDtypeStruct((n_exp * cap, d), BF16),
        compiler_params=_params(2),
        name="expert_ffn",
    )(idx_flat, hp, w_gate, w_up, w_down)


def _combine_kernel(ib_ref, ic_ref, fl_ref, o_ref, idx_ref, aff_ref, x1_ref, g2_ref, lg_ref, lb_ref, out_ref, acc,
                    *, tb, ck, nch):
    it = pl.program_id(0)
    flags = fl_ref[it]

    @pl.when((flags & 2) != 0)
    def _():
        acc[...] = jnp.zeros_like(acc)

    @pl.when((flags & 1) != 0)
    def _():
        tok = ib_ref[it] * tb + lax.broadcasted_iota(I32, (tb, ck), 0)
        onehot = jnp.where(tok == idx_ref[0], 1.0, 0.0).astype(BF16)
        aff = aff_ref[...]
        expert = ic_ref[it] // nch
        gate = jnp.sum(jnp.where(lax.broadcasted_iota(I32, aff.shape, 1) == expert, aff, 0.0),
                       axis=-1, keepdims=True)
        acc[...] += gate * jnp.dot(onehot, o_ref[...], preferred_element_type=F32)

    @pl.when((flags & 4) != 0)
    def _():
        z = DEEPNORM_ALPHA * x1_ref[...] + g2_ref[0:1, :] * acc[...]
        out_ref[...] = _ln_rows(z, LN_EPS) * lg_ref[...] + lb_ref[...]


def _combine(expert_out, idx, aff, x1, mods, ln_g, ln_b):
    n_exp, cap = idx.shape
    s_len, d = x1.shape
    tb = _pick(s_len, (512, 256, 128))
    ck = _pick(cap, (256, 128))
    nb, nch = s_len // tb, cap // ck
    n_items = n_exp * nch + n_exp * nb

    idx3 = idx.reshape(n_exp, nch, ck)
    lo = idx3[:, :, 0] // tb
    hi = idx3[:, :, ck - 1] // tb
    bb = jnp.arange(nb, dtype=I32)[:, None, None]
    hit = (lo[None] <= bb) & (bb <= hi[None])
    empty = ~jnp.any(hit, axis=(1, 2))
    hit = hit.at[:, 0, 0].set(hit[:, 0, 0] | empty)
    flat = hit.reshape(-1)
    (ids,) = jnp.nonzero(flat, size=n_items, fill_value=0)
    count = jnp.sum(flat.astype(I32))
    pos = jnp.arange(n_items, dtype=I32)
    valid = pos < count
    ids = jnp.where(valid, ids, ids[count - 1]).astype(I32)
    item_b = ids // (n_exp * nch)
    item_c = ids % (n_exp * nch)
    prev_b = jnp.concatenate([jnp.full((1,), -1, I32), item_b[:-1]])
    next_b = jnp.concatenate([item_b[1:], jnp.full((1,), -1, I32)])
    next_valid = jnp.concatenate([valid[1:], jnp.zeros((1,), bool)])
    first = valid & (item_b != prev_b)
    last = valid & ((item_b != next_b) | ~next_valid)
    flags = valid.astype(I32) + 2 * first.astype(I32) + 4 * last.astype(I32)

    grid_spec = pltpu.PrefetchScalarGridSpec(
        num_scalar_prefetch=3,
        grid=(n_items,),
        in_specs=[pl.BlockSpec((ck, d), lambda it, ib, ic, fl: (ic[it], 0)),
                  pl.BlockSpec((1, 1, ck), lambda it, ib, ic, fl: (ic[it], 0, 0)),
                  pl.BlockSpec((tb, n_exp), lambda it, ib, ic, fl: (ib[it], 0)),
                  pl.BlockSpec((tb, d), lambda it, ib, ic, fl: (ib[it], 0)),
                  pl.BlockSpec((SUBLANES, d), lambda it, ib, ic, fl: (0, 5)),
                  pl.BlockSpec((1, d), lambda it, ib, ic, fl: (0, 0)),
                  pl.BlockSpec((1, d), lambda it, ib, ic, fl: (0, 0))],
        out_specs=pl.BlockSpec((tb, d), lambda it, ib, ic, fl: (ib[it], 0)),
        scratch_shapes=[pltpu.VMEM((tb, d), F32)],
    )
    return pl.pallas_call(
        functools.partial(_combine_kernel, tb=tb, ck=ck, nch=nch),
        grid_spec=grid_spec,
        out_shape=jax.ShapeDtypeStruct((s_len, d), F32),
        compiler_params=_params(1),
        name="combine_ln2",
    )(item_b, item_c, flags, expert_out, idx.reshape(n_exp * nch, 1, ck), aff, x1, mods,
      ln_g.reshape(1, d), ln_b.reshape(1, d))


def kernel(x, c, ctx, c_ctx, w_mod, b_mod, w_in, diff_lambda, diff_subln_g, w_pool, pool_scale, w_branch_attn,
           w_branch_pool, w_out, ln1_g, ln1_b, w_router, w_gate, w_up, w_down, ln2_g, ln2_b):
    batch, s_len, d = x.shape
    assert batch == 1 and w_mod.shape[0] == DEPTH
    attn_w = N_HEADS * V_DIM
    pool_w = w_pool.shape[1] * w_pool.shape[2]
    k_off, v_off, p_off = attn_w, 2 * attn_w, 3 * attn_w
    g_off = p_off + pool_w
    n_exp = w_router.shape[-1]
    cap = EC_CAPACITY_FACTOR * s_len // n_exp

    mods = _modulation(c, c_ctx, w_mod[0], b_mod[0])
    h = _premod(x[0], mods, 0)
    hc = _premod(ctx[0], mods, 1)

    w = w_in[0]
    tabs = _rope_tables(s_len)
    q = _project(h, w, 0, attn_w, BF16, tabs, HEAD_DIM ** -0.5 * math.log2(math.e))
    k = _project(h, w, k_off, attn_w, BF16, tabs)
    v = _project(h, w, v_off, attn_w, BF16)
    kc = _project(hc, w, k_off, attn_w, BF16)
    vc = _project(hc, w, v_off, attn_w, BF16)
    p = _project(h, w, p_off, pool_w, F32)
    gates = _project(h, w, g_off, 2 * d, F32)

    attn = _attention(q, k, v, kc, vc, diff_lambda[0], diff_subln_g[0])
    pooled = _pool(p, w_pool[0], pool_scale[0])
    merged = _merge(attn, pooled, w_branch_attn[0], w_branch_pool[0], gates)
    y = _project(merged, w_out[0], 0, d, F32)

    x1, hp, aff_t = _post1(x[0], y, mods, ln1_g[0], ln1_b[0], w_router[0].T)
    idx = _route(aff_t, cap).reshape(n_exp, cap)
    expert_out = _expert_ffn(idx.reshape(-1), hp, w_gate[0], w_up[0], w_down[0])
    out = _combine(expert_out, idx, aff_t.T, x1, mods, ln2_g[0], ln2_b[0])
    return out.reshape(batch, s_len, d)
```

```python
import functools
import math

import jax
import jax.numpy as jnp
from jax import lax
from jax.experimental import pallas as pl
from jax.experimental.pallas import tpu as pltpu

F32 = jnp.float32
BF16 = jnp.bfloat16
I32 = jnp.int32
U32 = jnp.uint32

LANES = 128
SUBLANES = 8
VMEM_LIMIT = 56 * 1024 * 1024

GRID_W = 64
N_HEADS = 16
HEAD_DIM = 64
V_DIM = 2 * HEAD_DIM
POOL_WINDOWS = (2, 4, 8, 16)
POOL_HALO = 8
EC_CAPACITY_FACTOR = 2
ROPE_BASE = 10000.0
LN_EPS = 1e-5
ADALN_EPS = 1e-6
SUBLN_EPS = 1e-5
DEPTH = 1
DEEPNORM_ALPHA = (2.0 * DEPTH) ** 0.25
LAMBDA_INIT = 0.8 - 0.6 * math.exp(-0.3 * 0)
NEG_BIG = -1e30
BISECT_STEPS = 64
GATHER_UNROLL = 8
ATTN_STRIP = 32


def _pick(n, prefs):
    for p in prefs:
        if n % p == 0:
            return p
    raise ValueError(f"no tile for {n} in {prefs}")


def _params(n_axes):
    return pltpu.CompilerParams(dimension_semantics=("arbitrary",) * n_axes,
                                vmem_limit_bytes=VMEM_LIMIT)


def _ln_rows(x, eps):
    mu = jnp.mean(x, axis=-1, keepdims=True)
    xc = x - mu
    var = jnp.mean(xc * xc, axis=-1, keepdims=True)
    return xc * lax.rsqrt(var + eps)


def _mod_kernel(cb_ref, w_ref, b_ref, o_ref, *, tn):
    o_ref[...] = jnp.zeros_like(o_ref)
    for i in range(2):
        cv = cb_ref[i]
        s = jnp.tile(cv * jax.nn.sigmoid(cv), (1, tn // LANES))
        o_ref[i:i + 1, :] = jnp.sum(w_ref[...] * s, axis=0, keepdims=True) + b_ref[...]


def _modulation(c, c_ctx, w_mod, b_mod):
    d, n = w_mod.shape
    tn = _pick(n, (512, 256, 128))
    cb = jnp.broadcast_to(jnp.stack([c[0], c_ctx])[:, :, None], (2, d, LANES))
    return pl.pallas_call(
        functools.partial(_mod_kernel, tn=tn),
        grid=(n // tn,),
        in_specs=[pl.BlockSpec((2, d, LANES), lambda j: (0, 0, 0)),
                  pl.BlockSpec((d, tn), lambda j: (0, j)),
                  pl.BlockSpec((1, tn), lambda j: (0, j))],
        out_specs=pl.BlockSpec((SUBLANES, tn), lambda j: (0, j)),
        out_shape=jax.ShapeDtypeStruct((SUBLANES, n), F32),
        compiler_params=_params(1),
        name="modulation",
    )(cb, w_mod, b_mod.reshape(1, n))


def _premod_kernel(x_ref, sh_ref, sc_ref, o_ref, *, row):
    h = _ln_rows(x_ref[...], ADALN_EPS) * (1.0 + sc_ref[row:row + 1, :]) + sh_ref[row:row + 1, :]
    o_ref[...] = h.astype(BF16)


def _premod(x2d, mods, row):
    m, d = x2d.shape
    tm = _pick(m, (256, 128))
    return pl.pallas_call(
        functools.partial(_premod_kernel, row=row),
        grid=(m // tm,),
        in_specs=[pl.BlockSpec((tm, d), lambda i: (i, 0)),
                  pl.BlockSpec((SUBLANES, d), lambda i: (0, 0)),
                  pl.BlockSpec((SUBLANES, d), lambda i: (0, 1))],
        out_specs=pl.BlockSpec((tm, d), lambda i: (i, 0)),
        out_shape=jax.ShapeDtypeStruct((m, d), BF16),
        compiler_params=_params(1),
        name="premod",
    )(x2d, mods, mods)


def _proj_kernel(*refs, rope, scale, tn):
    if rope:
        h_ref, w_ref, cos_ref, sa_ref, sb_ref, o_ref, wbf = refs
    else:
        h_ref, w_ref, o_ref, wbf = refs

    @pl.when(pl.program_id(1) == 0)
    def _():
        wbf[...] = w_ref[...].astype(BF16)

    acc = jnp.dot(h_ref[...], wbf[...], preferred_element_type=F32)
    if rope:
        cos, sa, sb = cos_ref[...], sa_ref[...], sb_ref[...]
        for c in range(tn // LANES):
            cols = slice(c * LANES, (c + 1) * LANES)
            a = acc[:, cols]
            r = a * cos + pltpu.roll(a, LANES - 16, 1) * sa + pltpu.roll(a, 16, 1) * sb
            o_ref[:, cols] = (r * scale).astype(o_ref.dtype)
    else:
        o_ref[...] = acc.astype(o_ref.dtype)


def _project(h, w, col_off, n_cols, out_dtype, rope_tabs=None, scale=1.0):
    m, k = h.shape
    tn = _pick(math.gcd(n_cols, col_off), (512, 256, 128))
    tm = _pick(m, (1024, 512, 256, 128))
    off = col_off // tn
    in_specs = [pl.BlockSpec((tm, k), lambda j, i: (i, 0)),
                pl.BlockSpec((k, tn), lambda j, i: (0, j + off))]
    args = [h, w]
    if rope_tabs is not None:
        in_specs += [pl.BlockSpec((tm, LANES), lambda j, i: (i, 0))] * 3
        args += list(rope_tabs)
    return pl.pallas_call(
        functools.partial(_proj_kernel, rope=rope_tabs is not None, scale=scale, tn=tn),
        grid=(n_cols // tn, m // tm),
        in_specs=in_specs,
        out_specs=pl.BlockSpec((tm, tn), lambda j, i: (i, j)),
        out_shape=jax.ShapeDtypeStruct((m, n_cols), out_dtype),
        scratch_shapes=[pltpu.VMEM((k, tn), BF16)],
        compiler_params=_params(2),
        name="project",
    )(*args)


def _rope_tables(n_tokens):
    rows = n_tokens // GRID_W
    row = jnp.broadcast_to(jnp.arange(rows, dtype=F32)[:, None], (rows, GRID_W)).reshape(-1)
    col = jnp.broadcast_to(jnp.arange(GRID_W, dtype=F32)[None, :], (rows, GRID_W)).reshape(-1)
    axis_dim = HEAD_DIM // 2
    nf = axis_dim // 2
    inv_freq = ROPE_BASE ** (-jnp.arange(0, axis_dim, 2, dtype=F32) / axis_dim)
    ang = jnp.stack([row[:, None] * inv_freq, col[:, None] * inv_freq], axis=1)
    cos, sin = jnp.cos(ang), jnp.sin(ang)
    zero = jnp.zeros_like(sin)
    cos64 = jnp.concatenate([cos, cos], axis=-1).reshape(n_tokens, HEAD_DIM)
    sa64 = jnp.concatenate([-sin, zero], axis=-1).reshape(n_tokens, HEAD_DIM)
    sb64 = jnp.concatenate([zero, sin], axis=-1).reshape(n_tokens, HEAD_DIM)
    rep = LANES // HEAD_DIM
    return tuple(jnp.tile(t, (1, rep)) for t in (cos64, sa64, sb64))


def _attn_kernel(q_ref, k_ref, v_ref, kc_ref, vc_ref, dl_ref, g_ref, o_ref, qs_sc, vx_sc, sa_sc, sb_sc, sc_sc,
                 pa_sc, pb_sc, xa_sc, xb_sc, xc_sc, m_sc, acc_sc, *, tq, tk, n_chunks):
    s_len = n_chunks * tk
    c_len = kc_ref.shape[0]

    @pl.when(pl.program_id(1) == 0)
    def _():
        vx_sc[0:s_len, 0:V_DIM] = v_ref[...]
        vx_sc[s_len:s_len + c_len, 0:V_DIM] = vc_ref[...]
        vx_sc[:, V_DIM:2 * V_DIM] = jnp.ones((s_len + c_len, V_DIM), BF16)

    q = q_ref[...]
    lane = lax.broadcasted_iota(I32, q.shape, 1)
    zero = jnp.zeros_like(q)
    qs_sc[0:tq, :] = jnp.where(lane < HEAD_DIM, q, zero)
    qs_sc[tq:2 * tq, :] = jnp.where(lane >= HEAD_DIM, q, zero)
    m_sc[...] = jnp.full_like(m_sc, NEG_BIG)
    acc_sc[...] = jnp.zeros_like(acc_sc)

    def scores(s_ref, x_ref, kb):
        s = lax.dot_general(qs_sc[...], kb, (((1,), (1,)), ((), ())), preferred_element_type=F32)
        s_ref[...] = s
        x_ref[...] = jnp.broadcast_to(jnp.max(s, axis=-1, keepdims=True), x_ref.shape)

    def absorb(s_ref, x_ref, p_ref, v0):
        width = s_ref.shape[1]
        for r0 in range(0, 2 * tq, ATTN_STRIP):
            rows = slice(r0, r0 + ATTN_STRIP)
            m_prev = m_sc[rows, :]
            m_new = jnp.maximum(m_prev, x_ref[rows, :])
            x_ref[rows, :] = jnp.exp2(m_prev - m_new)
            m_sc[rows, :] = m_new
            p_ref[rows, 0:width] = jnp.exp2((s_ref[rows, :] - jnp.tile(m_new, (1, width // LANES))).astype(BF16))
        pv = jnp.dot(p_ref[:, 0:width], vx_sc[pl.ds(v0, width), :], preferred_element_type=F32)
        acc_sc[...] = jnp.tile(x_ref[...], (1, 2)) * acc_sc[...] + pv

    def kchunk(c):
        return k_ref[pl.ds(pl.multiple_of(c * tk, tk), tk), :]

    scores(sa_sc, xa_sc, kchunk(0))

    def body(j, carry):
        c = 2 * j
        scores(sb_sc, xb_sc, kchunk(c + 1))
        absorb(sa_sc, xa_sc, pa_sc, pl.multiple_of(c * tk, tk))
        scores(sa_sc, xa_sc, kchunk(c + 2))
        absorb(sb_sc, xb_sc, pb_sc, pl.multiple_of((c + 1) * tk, tk))
        return carry

    lax.fori_loop(0, (n_chunks - 2) // 2, body, 0)
    scores(sb_sc, xb_sc, kchunk(n_chunks - 1))
    absorb(sa_sc, xa_sc, pa_sc, (n_chunks - 2) * tk)
    scores(sc_sc, xc_sc, kc_ref[...])
    absorb(sb_sc, xb_sc, pb_sc, (n_chunks - 1) * tk)
    absorb(sc_sc, xc_sc, pa_sc, s_len)

    dl = dl_ref[...]
    lam = (jnp.exp(jnp.sum(dl[0:1] * dl[1:2], axis=-1, keepdims=True))
           - jnp.exp(jnp.sum(dl[2:3] * dl[3:4], axis=-1, keepdims=True)) + LAMBDA_INIT)
    o = (acc_sc[0:tq, 0:V_DIM] / acc_sc[0:tq, V_DIM:2 * V_DIM]
         - lam * (acc_sc[tq:2 * tq, 0:V_DIM] / acc_sc[tq:2 * tq, V_DIM:2 * V_DIM]))
    o = o * lax.rsqrt(jnp.mean(o * o, axis=-1, keepdims=True) + SUBLN_EPS)
    o_ref[...] = (o * g_ref[...] * (1.0 - LAMBDA_INIT)).astype(o_ref.dtype)


def _attention(q, k, v, kc, vc, diff_lambda, subln_g):
    s_len, width = q.shape
    c_len = kc.shape[0]
    tq = _pick(s_len, (512, 256, 128))
    tk = _pick(s_len // 2, (1024, 512, 256, 128))
    n_chunks = s_len // tk
    assert n_chunks % 2 == 0
    n_heads = width // V_DIM
    return pl.pallas_call(
        functools.partial(_attn_kernel, tq=tq, tk=tk, n_chunks=n_chunks),
        grid=(n_heads, s_len // tq),
        in_specs=[pl.BlockSpec((tq, V_DIM), lambda h, i: (i, h)),
                  pl.BlockSpec((s_len, V_DIM), lambda h, i: (0, h)),
                  pl.BlockSpec((s_len, V_DIM), lambda h, i: (0, h)),
                  pl.BlockSpec((c_len, V_DIM), lambda h, i: (0, h)),
                  pl.BlockSpec((c_len, V_DIM), lambda h, i: (0, h)),
                  pl.BlockSpec((4, HEAD_DIM), lambda h, i: (0, 0)),
                  pl.BlockSpec((1, V_DIM), lambda h, i: (0, 0))],
        out_specs=pl.BlockSpec((tq, V_DIM), lambda h, i: (i, h)),
        out_shape=jax.ShapeDtypeStruct((s_len, width), BF16),
        scratch_shapes=[pltpu.VMEM((2 * tq, V_DIM), BF16), pltpu.VMEM((s_len + c_len, 2 * V_DIM), BF16),
                        pltpu.VMEM((2 * tq, tk), F32), pltpu.VMEM((2 * tq, tk), F32),
                        pltpu.VMEM((2 * tq, c_len), F32)]
                       + [pltpu.VMEM((2 * tq, tk), BF16)] * 2
                       + [pltpu.VMEM((2 * tq, LANES), F32)] * 4
                       + [pltpu.VMEM((2 * tq, 2 * V_DIM), F32)],
        compiler_params=_params(2),
        name="diff_attention",
    )(q, k, v, kc, vc, diff_lambda, subln_g.reshape(1, V_DIM))


def _pool_kernel(p_ref, prev_ref, next_ref, w_ref, sc_ref, o_ref, ext, *, tm, n_tokens, gdim):
    i = pl.program_id(0)
    last = pl.num_programs(0) - 1
    ext[0:POOL_HALO, :] = jnp.where(i > 0, prev_ref[...], 0.0)
    ext[POOL_HALO:POOL_HALO + tm, :] = p_ref[...]
    ext[POOL_HALO + tm:2 * POOL_HALO + tm, :] = jnp.where(i < last, next_ref[...], 0.0)
    t = i * tm + lax.broadcasted_iota(I32, (tm, 1), 0)
    for g, w in enumerate(POOL_WINDOWS):
        cols = slice(g * gdim, (g + 1) * gdim)
        ssum = None
        for d in range(-(w // 2), w // 2):
            part = ext[POOL_HALO + d:POOL_HALO + d + tm, cols]
            ssum = part if ssum is None else ssum + part
        cnt = jnp.minimum(t + w // 2, n_tokens) - jnp.maximum(t - w // 2, 0)
        mean = ssum / cnt.astype(F32)
        mg = (mean - p_ref[:, cols]).astype(BF16)
        y = jnp.dot(mg, w_ref[g].astype(BF16), preferred_element_type=F32)
        o_ref[:, cols] = (y * sc_ref[:, cols]).astype(o_ref.dtype)


def _pool(p, w_pool, pool_scale):
    s_len, width = p.shape
    n_groups, gdim, _ = w_pool.shape
    assert n_groups == len(POOL_WINDOWS) and max(POOL_WINDOWS) // 2 <= POOL_HALO
    tm = _pick(s_len, (512, 256, 128))
    hb = tm // POOL_HALO
    n_halo_blocks = s_len // POOL_HALO
    return pl.pallas_call(
        functools.partial(_pool_kernel, tm=tm, n_tokens=s_len, gdim=gdim),
        grid=(s_len // tm,),
        in_specs=[pl.BlockSpec((tm, width), lambda i: (i, 0)),
                  pl.BlockSpec((POOL_HALO, width), lambda i: (jnp.maximum(i * hb - 1, 0), 0)),
                  pl.BlockSpec((POOL_HALO, width), lambda i: (jnp.minimum((i + 1) * hb, n_halo_blocks - 1), 0)),
                  pl.BlockSpec((n_groups, gdim, gdim), lambda i: (0, 0, 0)),
                  pl.BlockSpec((1, width), lambda i: (0, 0))],
        out_specs=pl.BlockSpec((tm, width), lambda i: (i, 0)),
        out_shape=jax.ShapeDtypeStruct((s_len, width), BF16),
        scratch_shapes=[pltpu.VMEM((tm + 2 * POOL_HALO, width), F32)],
        compiler_params=_params(1),
        name="pool_mixer",
    )(p, p, p, w_pool, pool_scale.reshape(1, width))


def _merge_kernel(a_ref, p_ref, wa_ref, wb_ref, ga_ref, gb_ref, o_ref, wabf, wbbf):
    @pl.when(pl.program_id(1) == 0)
    def _():
        wabf[...] = wa_ref[...].astype(BF16)
        wbbf[...] = wb_ref[...].astype(BF16)

    ya = jnp.dot(a_ref[...], wabf[...], preferred_element_type=F32)
    yb = jnp.dot(p_ref[...], wbbf[...], preferred_element_type=F32)
    o_ref[...] = (jax.nn.sigmoid(ga_ref[...]) * ya + jax.nn.sigmoid(gb_ref[...]) * yb).astype(o_ref.dtype)


def _merge(attn, pooled, w_a, w_b, gates):
    m, ka = attn.shape
    kb = pooled.shape[1]
    n = w_a.shape[1]
    tn = _pick(n, (512, 256, 128))
    tm = _pick(m, (1024, 512, 256, 128))
    gb_off = n // tn
    return pl.pallas_call(
        _merge_kernel,
        grid=(n // tn, m // tm),
        in_specs=[pl.BlockSpec((tm, ka), lambda j, i: (i, 0)),
                  pl.BlockSpec((tm, kb), lambda j, i: (i, 0)),
                  pl.BlockSpec((ka, tn), lambda j, i: (0, j)),
                  pl.BlockSpec((kb, tn), lambda j, i: (0, j)),
                  pl.BlockSpec((tm, tn), lambda j, i: (i, j)),
                  pl.BlockSpec((tm, tn), lambda j, i: (i, j + gb_off))],
        out_specs=pl.BlockSpec((tm, tn), lambda j, i: (i, j)),
        out_shape=jax.ShapeDtypeStruct((m, n), BF16),
        scratch_shapes=[pltpu.VMEM((ka, tn), BF16), pltpu.VMEM((kb, tn), BF16)],
        compiler_params=_params(2),
        name="gated_merge",
    )(attn, pooled, w_a, w_b, gates, gates)


def _post1_kernel(x_ref, y_ref, g1_ref, sh_ref, sc_ref, lg_ref, lb_ref, wr_ref, x1_ref, hp_ref, aff_ref, *, half):
    z = DEEPNORM_ALPHA * x_ref[...] + g1_ref[0:1, :] * y_ref[...]
    x1 = _ln_rows(z, LN_EPS) * lg_ref[...] + lb_ref[...]
    x1_ref[...] = x1
    h2 = _ln_rows(x1, ADALN_EPS) * (1.0 + sc_ref[0:1, :]) + sh_ref[0:1, :]
    hb = lax.bitcast_convert_type(h2.astype(BF16).astype(F32), U32)
    hp_ref[...] = (hb[:, :half] >> 16) | (hb[:, half:] & jnp.uint32(0xFFFF0000))
    logits = lax.dot_general(wr_ref[...], h2, (((1,), (1,)), ((), ())),
                             precision=lax.Precision.HIGHEST, preferred_element_type=F32)
    mx = jnp.max(logits, axis=0, keepdims=True)
    ex = jnp.exp(logits - mx)
    aff_ref[...] = ex / jnp.sum(ex, axis=0, keepdims=True)


def _post1(x2d, y, mods, ln_g, ln_b, w_router_t):
    m, d = x2d.shape
    n_exp = w_router_t.shape[0]
    tm = _pick(m, (256, 128))
    row = lambda i: (i, 0)
    vec = lambda i: (0, 0)
    return pl.pallas_call(
        functools.partial(_post1_kernel, half=d // 2),
        grid=(m // tm,),
        in_specs=[pl.BlockSpec((tm, d), row), pl.BlockSpec((tm, d), row),
                  pl.BlockSpec((SUBLANES, d), lambda i: (0, 2)),
                  pl.BlockSpec((SUBLANES, d), lambda i: (0, 3)),
                  pl.BlockSpec((SUBLANES, d), lambda i: (0, 4)),
                  pl.BlockSpec((1, d), vec), pl.BlockSpec((1, d), vec),
                  pl.BlockSpec((n_exp, d), vec)],
        out_specs=[pl.BlockSpec((tm, d), row), pl.BlockSpec((tm, d // 2), row),
                   pl.BlockSpec((n_exp, tm), lambda i: (0, i))],
        out_shape=[jax.ShapeDtypeStruct((m, d), F32), jax.ShapeDtypeStruct((m, d // 2), U32),
                   jax.ShapeDtypeStruct((n_exp, m), F32)],
        compiler_params=_params(1),
        name="ln1_router",
    )(x2d, y, mods, mods, mods, ln_g.reshape(1, d), ln_b.reshape(1, d), w_router_t)


def _route_kernel(aff_ref, idx_ref, csel_sc, *, n_exp, nc, cap, sblk):
    rows = n_exp * nc
    aff3 = aff_ref[...]
    ri = lax.broadcasted_iota(I32, (rows, rows), 0)
    rj = lax.broadcasted_iota(I32, (rows, rows), 1)
    before = jnp.where(((ri // nc) == (rj // nc)) & (rj < ri), 1.0, 0.0).astype(BF16)
    li = lax.broadcasted_iota(I32, (LANES, LANES), 0)
    lj = lax.broadcasted_iota(I32, (LANES, LANES), 1)
    tri = jnp.where(li <= lj, 1.0, 0.0).astype(BF16)
    ones = jnp.ones((LANES, LANES), BF16)

    def per_expert_sum(x3):
        return jnp.sum(jnp.sum(x3, axis=1, keepdims=True), axis=2, keepdims=True)

    def count(mask3):
        return per_expert_sum(jnp.where(mask3, 1.0, 0.0))

    def prefix(mask2):
        mb = jnp.where(mask2, 1.0, 0.0).astype(BF16)
        within = jnp.dot(mb, tri, preferred_element_type=F32)
        row_tot = jnp.dot(mb, ones, preferred_element_type=F32).astype(BF16)
        return within + jnp.dot(before, row_tot, preferred_element_type=F32)

    def rows_of(x3):
        return jnp.broadcast_to(x3, (n_exp, nc, LANES)).reshape(rows, LANES)

    top = jnp.max(jnp.max(aff3, axis=1, keepdims=True), axis=2, keepdims=True)
    lo0 = jnp.zeros_like(top)
    hi0 = 2.0 * top + 1e-30

    def bisect(_, carry):
        lo, hi = carry
        mid = 0.5 * (lo + hi)
        ok = count(aff3 >= mid) >= cap
        return jnp.where(ok, mid, lo), jnp.where(ok, hi, mid)

    thr3, _ = lax.fori_loop(0, BISECT_STEPS, bisect, (lo0, hi0))
    need = rows_of(cap - count(aff3 > thr3))
    aff2 = aff3.reshape(rows, LANES)
    thr2 = rows_of(thr3)
    gt = aff2 > thr2
    eq = aff2 == thr2
    sel = gt | (eq & (prefix(eq) <= need))
    csel_sc[...] = jnp.where(sel, prefix(sel), 0.0)

    lane_i = lax.broadcasted_iota(I32, (1, LANES), 1)
    slot0 = lax.broadcasted_iota(I32, (sblk, LANES), 0).astype(F32) + 1.0

    def per_expert(e, carry):
        for sb in range(cap // sblk):
            slot = slot0 + float(sb * sblk)

            def per_chunk(c, acc):
                hit = csel_sc[pl.ds(e * nc + c, 1), :] == slot
                return acc + jnp.where(hit, (lane_i + c * LANES).astype(F32), 0.0)

            acc = lax.fori_loop(0, nc, per_chunk, jnp.zeros((sblk, LANES), F32))
            idx_ref[e, sb * sblk:(sb + 1) * sblk, :] = jnp.sum(acc, axis=-1, keepdims=True).astype(I32)
        return carry

    lax.fori_loop(0, n_exp, per_expert, 0)


def _route(aff_t, cap):
    n_exp, s_len = aff_t.shape
    nc = s_len // LANES
    rows = n_exp * nc
    sblk = _pick(cap, (256, 128))
    return pl.pallas_call(
        functools.partial(_route_kernel, n_exp=n_exp, nc=nc, cap=cap, sblk=sblk),
        grid=(1,),
        in_specs=[pl.BlockSpec((n_exp, nc, LANES), lambda i: (0, 0, 0))],
        out_specs=pl.BlockSpec((n_exp, cap, 1), lambda i: (0, 0, 0)),
        out_shape=jax.ShapeDtypeStruct((n_exp, cap, 1), I32),
        scratch_shapes=[pltpu.VMEM((rows, LANES), F32)],
        compiler_params=_params(1),
        name="expert_choice",
    )(aff_t.reshape(n_exp, nc, LANES))


def _ffn_kernel(idx_ref, hp_hbm, wg_ref, wu_ref, wd_ref, o_ref, hsel, hb, act, sem,
                *, n_exp, cap, nf, tf, half):
    e = pl.program_id(0)
    s = pl.program_id(1)

    def start_gather(ee):
        def go(r, carry):
            t = idx_ref[ee * cap + r]
            pltpu.make_async_copy(hp_hbm.at[pl.ds(t, 1), :], hsel.at[pl.ds(r, 1), :], sem.at[0]).start()
            return carry
        lax.fori_loop(0, cap, go, 0, unroll=GATHER_UNROLL)

    def wait_gather():
        pltpu.make_async_copy(hp_hbm.at[pl.ds(0, cap), :], hsel, sem.at[0]).wait()

    @pl.when((e == 0) & (s == 0))
    def _():
        start_gather(0)

    @pl.when(s == 0)
    def _():
        wait_gather()
        u = hsel[...]
        hb[:, :half] = lax.bitcast_convert_type(u << 16, F32).astype(BF16)
        hb[:, half:] = lax.bitcast_convert_type(u & jnp.uint32(0xFFFF0000), F32).astype(BF16)

        @pl.when(e + 1 < n_exp)
        def _():
            start_gather(e + 1)

    @pl.when(s < nf)
    def _():
        x = hb[...]
        u = jnp.dot(x, wg_ref[0].astype(BF16), preferred_element_type=F32)
        v = jnp.dot(x, wu_ref[0].astype(BF16), preferred_element_type=F32)
        act[s] = (u * jax.nn.sigmoid(u) * v).astype(BF16)

    @pl.when(s >= nf)
    def _():
        o = jnp.dot(act[0], wd_ref[0, 0:tf, :].astype(BF16), preferred_element_type=F32)
        for f in range(1, nf):
            o += jnp.dot(act[f], wd_ref[0, f * tf:(f + 1) * tf, :].astype(BF16), preferred_element_type=F32)
        o_ref[...] = o.astype(o_ref.dtype)


def _expert_ffn(idx_flat, hp, w_gate, w_up, w_down):
    n_exp, d, ff = w_gate.shape
    cap = idx_flat.shape[0] // n_exp
    half = hp.shape[1]
    tf = _pick(ff, (256, 128))
    tn = _pick(d, (512, 256, 128))
    nf, nn = ff // tf, d // tn
    last_e = n_exp - 1

    def up_map(e, s, idx):
        return (jnp.where(s < nf, e, jnp.minimum(e + 1, last_e)), 0, jnp.where(s < nf, s, 0))

    def down_map(e, s, idx):
        return (e, 0, jnp.maximum(s - nf, 0))

    def out_map(e, s, idx):
        return (e, jnp.maximum(s - nf, 0))

    grid_spec = pltpu.PrefetchScalarGridSpec(
        num_scalar_prefetch=1,
        grid=(n_exp, nf + nn),
        in_specs=[pl.BlockSpec(memory_space=pl.ANY),
                  pl.BlockSpec((1, d, tf), up_map),
                  pl.BlockSpec((1, d, tf), up_map),
                  pl.BlockSpec((1, ff, tn), down_map)],
        out_specs=pl.BlockSpec((cap, tn), out_map),
        scratch_shapes=[pltpu.VMEM((cap, half), U32), pltpu.VMEM((cap, d), BF16),
                        pltpu.VMEM((nf, cap, tf), BF16), pltpu.SemaphoreType.DMA((1,))],
    )
    return pl.pallas_call(
        functools.partial(_ffn_kernel, n_exp=n_exp, cap=cap, nf=nf, tf=tf, half=half),
        grid_spec=grid_spec,
        out_shape=jax.ShapeDtypeStruct((n_exp * cap, d), BF16),
        compiler_params=_params(2),
        name="expert_ffn",
    )(idx_flat, hp, w_gate, w_up, w_down)


def _combine_kernel(ib_ref, ic_ref, fl_ref, o_ref, idx_ref, aff_ref, x1_ref, g2_ref, lg_ref, lb_ref, out_ref, acc,
                    *, tb, ck, nch):
    it = pl.program_id(0)
    flags = fl_ref[it]

    @pl.when((flags & 2) != 0)
    def _():
        acc[...] = jnp.zeros_like(acc)

    @pl.when((flags & 1) != 0)
    def _():
        tok = ib_ref[it] * tb + lax.broadcasted_iota(I32, (tb, ck), 0)
        onehot = jnp.where(tok == idx_ref[0], 1.0, 0.0).astype(BF16)
        aff = aff_ref[...]
        expert = ic_ref[it] // nch
        gate = jnp.sum(jnp.where(lax.broadcasted_iota(I32, aff.shape, 1) == expert, aff, 0.0),
                       axis=-1, keepdims=True)
        acc[...] += gate * jnp.dot(onehot, o_ref[...], preferred_element_type=F32)

    @pl.when((flags & 4) != 0)
    def _():
        z = DEEPNORM_ALPHA * x1_ref[...] + g2_ref[0:1, :] * acc[...]
        out_ref[...] = _ln_rows(z, LN_EPS) * lg_ref[...] + lb_ref[...]


def _combine(expert_out, idx, aff, x1, mods, ln_g, ln_b):
    n_exp, cap = idx.shape
    s_len, d = x1.shape
    tb = _pick(s_len, (512, 256, 128))
    ck = _pick(cap, (256, 128))
    nb, nch = s_len // tb, cap // ck
    n_items = n_exp * nch + n_exp * nb

    idx3 = idx.reshape(n_exp, nch, ck)
    lo = idx3[:, :, 0] // tb
    hi = idx3[:, :, ck - 1] // tb
    bb = jnp.arange(nb, dtype=I32)[:, None, None]
    hit = (lo[None] <= bb) & (bb <= hi[None])
    empty = ~jnp.any(hit, axis=(1, 2))
    hit = hit.at[:, 0, 0].set(hit[:, 0, 0] | empty)
    flat = hit.reshape(-1)
    (ids,) = jnp.nonzero(flat, size=n_items, fill_value=0)
    count = jnp.sum(flat.astype(I32))
    pos = jnp.arange(n_items, dtype=I32)
    valid = pos < count
    ids = jnp.where(valid, ids, ids[count - 1]).astype(I32)
    item_b = ids // (n_exp * nch)
    item_c = ids % (n_exp * nch)
    prev_b = jnp.concatenate([jnp.full((1,), -1, I32), item_b[:-1]])
    next_b = jnp.concatenate([item_b[1:], jnp.full((1,), -1, I32)])
    next_valid = jnp.concatenate([valid[1:], jnp.zeros((1,), bool)])
    first = valid & (item_b != prev_b)
    last = valid & ((item_b != next_b) | ~next_valid)
    flags = valid.astype(I32) + 2 * first.astype(I32) + 4 * last.astype(I32)

    grid_spec = pltpu.PrefetchScalarGridSpec(
        num_scalar_prefetch=3,
        grid=(n_items,),
        in_specs=[pl.BlockSpec((ck, d), lambda it, ib, ic, fl: (ic[it], 0)),
                  pl.BlockSpec((1, 1, ck), lambda it, ib, ic, fl: (ic[it], 0, 0)),
                  pl.BlockSpec((tb, n_exp), lambda it, ib, ic, fl: (ib[it], 0)),
                  pl.BlockSpec((tb, d), lambda it, ib, ic, fl: (ib[it], 0)),
                  pl.BlockSpec((SUBLANES, d), lambda it, ib, ic, fl: (0, 5)),
                  pl.BlockSpec((1, d), lambda it, ib, ic, fl: (0, 0)),
                  pl.BlockSpec((1, d), lambda it, ib, ic, fl: (0, 0))],
        out_specs=pl.BlockSpec((tb, d), lambda it, ib, ic, fl: (ib[it], 0)),
        scratch_shapes=[pltpu.VMEM((tb, d), F32)],
    )
    return pl.pallas_call(
        functools.partial(_combine_kernel, tb=tb, ck=ck, nch=nch),
        grid_spec=grid_spec,
        out_shape=jax.ShapeDtypeStruct((s_len, d), F32),
        compiler_params=_params(1),
        name="combine_ln2",
    )(item_b, item_c, flags, expert_out, idx.reshape(n_exp * nch, 1, ck), aff, x1, mods,
      ln_g.reshape(1, d), ln_b.reshape(1, d))


def kernel(x, c, ctx, c_ctx, w_mod, b_mod, w_in, diff_lambda, diff_subln_g, w_pool, pool_scale, w_branch_attn,
           w_branch_pool, w_out, ln1_g, ln1_b, w_router, w_gate, w_up, w_down, ln2_g, ln2_b):
    batch, s_len, d = x.shape
    assert batch == 1 and w_mod.shape[0] == DEPTH
    attn_w = N_HEADS * V_DIM
    pool_w = w_pool.shape[1] * w_pool.shape[2]
    k_off, v_off, p_off = attn_w, 2 * attn_w, 3 * attn_w
    g_off = p_off + pool_w
    n_exp = w_router.shape[-1]
    cap = EC_CAPACITY_FACTOR * s_len // n_exp

    mods = _modulation(c, c_ctx, w_mod[0], b_mod[0])
    h = _premod(x[0], mods, 0)
    hc = _premod(ctx[0], mods, 1)

    w = w_in[0]
    tabs = _rope_tables(s_len)
    q = _project(h, w, 0, attn_w, BF16, tabs, HEAD_DIM ** -0.5 * math.log2(math.e))
    k = _project(h, w, k_off, attn_w, BF16, tabs)
    v = _project(h, w, v_off, attn_w, BF16)
    kc = _project(hc, w, k_off, attn_w, BF16)
    vc = _project(hc, w, v_off, attn_w, BF16)
    p = _project(h, w, p_off, pool_w, F32)
    gates = _project(h, w, g_off, 2 * d, F32)

    attn = _attention(q, k, v, kc, vc, diff_lambda[0], diff_subln_g[0])
    pooled = _pool(p, w_pool[0], pool_scale[0])
    merged = _merge(attn, pooled, w_branch_attn[0], w_branch_pool[0], gates)
    y = _project(merged, w_out[0], 0, d, F32)

    x1, hp, aff_t = _post1(x[0], y, mods, ln1_g[0], ln1_b[0], w_router[0].T)
    idx = _route(aff_t, cap).reshape(n_exp, cap)
    expert_out = _expert_ffn(idx.reshape(-1), hp, w_gate[0], w_up[0], w_down[0])
    out = _combine(expert_out, idx, aff_t.T, x1, mods, ln2_g[0], ln2_b[0])
    return out.reshape(batch, s_len, d)
```

```python
import functools
import math

import jax
import jax.numpy as jnp
from jax import lax
from jax.experimental import pallas as pl
from jax.experimental.pallas import tpu as pltpu

F32 = jnp.float32
BF16 = jnp.bfloat16
I32 = jnp.int32
U32 = jnp.uint32

LANES = 128
SUBLANES = 8
VMEM_LIMIT = 56 * 1024 * 1024

GRID_W = 64
N_HEADS = 16
HEAD_DIM = 64
V_DIM = 2 * HEAD_DIM
POOL_WINDOWS = (2, 4, 8, 16)
POOL_HALO = 8
EC_CAPACITY_FACTOR = 2
ROPE_BASE = 10000.0
LN_EPS = 1e-5
ADALN_EPS = 1e-6
SUBLN_EPS = 1e-5
DEPTH = 1
DEEPNORM_ALPHA = (2.0 * DEPTH) ** 0.25
LAMBDA_INIT = 0.8 - 0.6 * math.exp(-0.3 * 0)
NEG_BIG = -1e30
BISECT_STEPS = 64
GATHER_UNROLL = 8
ROPE_ROW_PARTS = 8
LN_STRIP = 16
ATTN_STRIP = 32


def _pick(n, prefs):
    for p in prefs:
        if n % p == 0:
            return p
    raise ValueError(f"no tile for {n} in {prefs}")


def _params(n_axes):
    return pltpu.CompilerParams(dimension_semantics=("arbitrary",) * n_axes,
                                vmem_limit_bytes=VMEM_LIMIT)


def _ln_rows(x, eps):
    mu = jnp.mean(x, axis=-1, keepdims=True)
    xc = x - mu
    var = jnp.mean(xc * xc, axis=-1, keepdims=True)
    return xc * lax.rsqrt(var + eps)


def _mod_kernel(cb_ref, w_ref, b_ref, o_ref, *, tn):
    o_ref[...] = jnp.zeros_like(o_ref)
    for i in range(2):
        cv = cb_ref[i]
        s = jnp.tile(cv * jax.nn.sigmoid(cv), (1, tn // LANES))
        o_ref[i:i + 1, :] = jnp.sum(w_ref[...] * s, axis=0, keepdims=True) + b_ref[...]


def _modulation(c, c_ctx, w_mod, b_mod):
    d, n = w_mod.shape
    tn = _pick(n, (512, 256, 128))
    cb = jnp.broadcast_to(jnp.stack([c[0], c_ctx])[:, :, None], (2, d, LANES))
    return pl.pallas_call(
        functools.partial(_mod_kernel, tn=tn),
        grid=(n // tn,),
        in_specs=[pl.BlockSpec((2, d, LANES), lambda j: (0, 0, 0)),
                  pl.BlockSpec((d, tn), lambda j: (0, j)),
                  pl.BlockSpec((1, tn), lambda j: (0, j))],
        out_specs=pl.BlockSpec((SUBLANES, tn), lambda j: (0, j)),
        out_shape=jax.ShapeDtypeStruct((SUBLANES, n), F32),
        compiler_params=_params(1),
        name="modulation",
    )(cb, w_mod, b_mod.reshape(1, n))


def _premod_kernel(x_ref, sh_ref, sc_ref, o_ref, *, row):
    scale = 1.0 + sc_ref[row:row + 1, :]
    shift = sh_ref[row:row + 1, :]
    for r0 in range(0, x_ref.shape[0], LN_STRIP):
        rows = slice(r0, r0 + LN_STRIP)
        o_ref[rows, :] = (_ln_rows(x_ref[rows, :], ADALN_EPS) * scale + shift).astype(BF16)


def _premod(x2d, mods, row):
    m, d = x2d.shape
    tm = _pick(m, (256, 128))
    return pl.pallas_call(
        functools.partial(_premod_kernel, row=row),
        grid=(m // tm,),
        in_specs=[pl.BlockSpec((tm, d), lambda i: (i, 0)),
                  pl.BlockSpec((SUBLANES, d), lambda i: (0, 0)),
                  pl.BlockSpec((SUBLANES, d), lambda i: (0, 1))],
        out_specs=pl.BlockSpec((tm, d), lambda i: (i, 0)),
        out_shape=jax.ShapeDtypeStruct((m, d), BF16),
        compiler_params=_params(1),
        name="premod",
    )(x2d, mods, mods)


def _proj_kernel(*refs, rope, scale, tn):
    if rope:
        h_ref, w_ref, cos_ref, sa_ref, sb_ref, o_ref, wbf = refs
    else:
        h_ref, w_ref, o_ref, wbf = refs

    @pl.when(pl.program_id(1) == 0)
    def _():
        wbf[...] = w_ref[...].astype(BF16)

    if rope:
        half = h_ref.shape[0] // ROPE_ROW_PARTS
        for r0 in range(0, h_ref.shape[0], half):
            rows = slice(r0, r0 + half)
            acc = jnp.dot(h_ref[rows, :], wbf[...], preferred_element_type=F32)
            cos, sa, sb = cos_ref[rows, :], sa_ref[rows, :], sb_ref[rows, :]
            for c in range(tn // LANES):
                cols = slice(c * LANES, (c + 1) * LANES)
                a = acc[:, cols]
                r = a * cos + pltpu.roll(a, LANES - 16, 1) * sa + pltpu.roll(a, 16, 1) * sb
                o_ref[rows, cols] = (r * scale).astype(o_ref.dtype)
    else:
        o_ref[...] = jnp.dot(h_ref[...], wbf[...], preferred_element_type=F32).astype(o_ref.dtype)


def _project(h, w, col_off, n_cols, out_dtype, rope_tabs=None, scale=1.0):
    m, k = h.shape
    tn = _pick(math.gcd(n_cols, col_off), (512, 256, 128))
    tm = _pick(m, (1024, 512, 256, 128))
    off = col_off // tn
    in_specs = [pl.BlockSpec((tm, k), lambda j, i: (i, 0)),
                pl.BlockSpec((k, tn), lambda j, i: (0, j + off))]
    args = [h, w]
    if rope_tabs is not None:
        in_specs += [pl.BlockSpec((tm, LANES), lambda j, i: (i, 0))] * 3
        args += list(rope_tabs)
    return pl.pallas_call(
        functools.partial(_proj_kernel, rope=rope_tabs is not None, scale=scale, tn=tn),
        grid=(n_cols // tn, m // tm),
        in_specs=in_specs,
        out_specs=pl.BlockSpec((tm, tn), lambda j, i: (i, j)),
        out_shape=jax.ShapeDtypeStruct((m, n_cols), out_dtype),
        scratch_shapes=[pltpu.VMEM((k, tn), BF16)],
        compiler_params=_params(2),
        name="project",
    )(*args)


def _rope_tables(n_tokens):
    rows = n_tokens // GRID_W
    row = jnp.broadcast_to(jnp.arange(rows, dtype=F32)[:, None], (rows, GRID_W)).reshape(-1)
    col = jnp.broadcast_to(jnp.arange(GRID_W, dtype=F32)[None, :], (rows, GRID_W)).reshape(-1)
    axis_dim = HEAD_DIM // 2
    nf = axis_dim // 2
    inv_freq = ROPE_BASE ** (-jnp.arange(0, axis_dim, 2, dtype=F32) / axis_dim)
    ang = jnp.stack([row[:, None] * inv_freq, col[:, None] * inv_freq], axis=1)
    cos, sin = jnp.cos(ang), jnp.sin(ang)
    zero = jnp.zeros_like(sin)
    cos64 = jnp.concatenate([cos, cos], axis=-1).reshape(n_tokens, HEAD_DIM)
    sa64 = jnp.concatenate([-sin, zero], axis=-1).reshape(n_tokens, HEAD_DIM)
    sb64 = jnp.concatenate([zero, sin], axis=-1).reshape(n_tokens, HEAD_DIM)
    rep = LANES // HEAD_DIM
    return tuple(jnp.tile(t, (1, rep)) for t in (cos64, sa64, sb64))


def _attn_kernel(q_ref, k_ref, v_ref, kc_ref, vc_ref, dl_ref, g_ref, o_ref, qs_sc, vx_sc, sa_sc, sb_sc, sc_sc,
                 pa_sc, pb_sc, xa_sc, xb_sc, xc_sc, m_sc, acc_sc, *, tq, tk, n_chunks):
    s_len = n_chunks * tk
    c_len = kc_ref.shape[0]

    @pl.when(pl.program_id(1) == 0)
    def _():
        vx_sc[0:s_len, 0:V_DIM] = v_ref[...]
        vx_sc[s_len:s_len + c_len, 0:V_DIM] = vc_ref[...]
        vx_sc[:, V_DIM:2 * V_DIM] = jnp.ones((s_len + c_len, V_DIM), BF16)

    q = q_ref[...]
    lane = lax.broadcasted_iota(I32, q.shape, 1)
    zero = jnp.zeros_like(q)
    qs_sc[0:tq, :] = jnp.where(lane < HEAD_DIM, q, zero)
    qs_sc[tq:2 * tq, :] = jnp.where(lane >= HEAD_DIM, q, zero)
    m_sc[...] = jnp.full_like(m_sc, NEG_BIG)
    acc_sc[...] = jnp.zeros_like(acc_sc)

    def scores(s_ref, x_ref, kb):
        s = lax.dot_general(qs_sc[...], kb, (((1,), (1,)), ((), ())), preferred_element_type=F32)
        s_ref[...] = s
        x_ref[...] = jnp.broadcast_to(jnp.max(s, axis=-1, keepdims=True), x_ref.shape)

    def absorb(s_ref, x_ref, p_ref, v0):
        width = s_ref.shape[1]
        for r0 in range(0, 2 * tq, ATTN_STRIP):
            rows = slice(r0, r0 + ATTN_STRIP)
            m_prev = m_sc[rows, :]
            m_new = jnp.maximum(m_prev, x_ref[rows, :])
            x_ref[rows, :] = jnp.exp2(m_prev - m_new)
            m_sc[rows, :] = m_new
            p_ref[rows, 0:width] = jnp.exp2((s_ref[rows, :] - jnp.tile(m_new, (1, width // LANES))).astype(BF16))
        pv = jnp.dot(p_ref[:, 0:width], vx_sc[pl.ds(v0, width), :], preferred_element_type=F32)
        acc_sc[...] = jnp.tile(x_ref[...], (1, 2)) * acc_sc[...] + pv

    def kchunk(c):
        return k_ref[pl.ds(pl.multiple_of(c * tk, tk), tk), :]

    scores(sa_sc, xa_sc, kchunk(0))

    def body(j, carry):
        c = 2 * j
        scores(sb_sc, xb_sc, kchunk(c + 1))
        absorb(sa_sc, xa_sc, pa_sc, pl.multiple_of(c * tk, tk))
        scores(sa_sc, xa_sc, kchunk(c + 2))
        absorb(sb_sc, xb_sc, pb_sc, pl.multiple_of((c + 1) * tk, tk))
        return carry

    lax.fori_loop(0, (n_chunks - 2) // 2, body, 0)
    scores(sb_sc, xb_sc, kchunk(n_chunks - 1))
    absorb(sa_sc, xa_sc, pa_sc, (n_chunks - 2) * tk)
    scores(sc_sc, xc_sc, kc_ref[...])
    absorb(sb_sc, xb_sc, pb_sc, (n_chunks - 1) * tk)
    absorb(sc_sc, xc_sc, pa_sc, s_len)

    dl = dl_ref[...]
    lam = (jnp.exp(jnp.sum(dl[0:1] * dl[1:2], axis=-1, keepdims=True))
           - jnp.exp(jnp.sum(dl[2:3] * dl[3:4], axis=-1, keepdims=True)) + LAMBDA_INIT)
    o = (acc_sc[0:tq, 0:V_DIM] / acc_sc[0:tq, V_DIM:2 * V_DIM]
         - lam * (acc_sc[tq:2 * tq, 0:V_DIM] / acc_sc[tq:2 * tq, V_DIM:2 * V_DIM]))
    o = o * lax.rsqrt(jnp.mean(o * o, axis=-1, keepdims=True) + SUBLN_EPS)
    o_ref[...] = (o * g_ref[...] * (1.0 - LAMBDA_INIT)).astype(o_ref.dtype)


def _attention(q, k, v, kc, vc, diff_lambda, subln_g):
    s_len, width = q.shape
    c_len = kc.shape[0]
    tq = _pick(s_len, (512, 256, 128))
    tk = _pick(s_len // 2, (2048, 1024, 512, 256, 128))
    n_chunks = s_len // tk
    assert n_chunks % 2 == 0
    n_heads = width // V_DIM
    return pl.pallas_call(
        functools.partial(_attn_kernel, tq=tq, tk=tk, n_chunks=n_chunks),
        grid=(n_heads, s_len // tq),
        in_specs=[pl.BlockSpec((tq, V_DIM), lambda h, i: (i, h)),
                  pl.BlockSpec((s_len, V_DIM), lambda h, i: (0, h)),
                  pl.BlockSpec((s_len, V_DIM), lambda h, i: (0, h)),
                  pl.BlockSpec((c_len, V_DIM), lambda h, i: (0, h)),
                  pl.BlockSpec((c_len, V_DIM), lambda h, i: (0, h)),
                  pl.BlockSpec((4, HEAD_DIM), lambda h, i: (0, 0)),
                  pl.BlockSpec((1, V_DIM), lambda h, i: (0, 0))],
        out_specs=pl.BlockSpec((tq, V_DIM), lambda h, i: (i, h)),
        out_shape=jax.ShapeDtypeStruct((s_len, width), BF16),
        scratch_shapes=[pltpu.VMEM((2 * tq, V_DIM), BF16), pltpu.VMEM((s_len + c_len, 2 * V_DIM), BF16),
                        pltpu.VMEM((2 * tq, tk), F32), pltpu.VMEM((2 * tq, tk), F32),
                        pltpu.VMEM((2 * tq, c_len), F32)]
                       + [pltpu.VMEM((2 * tq, tk), BF16)] * 2
                       + [pltpu.VMEM((2 * tq, LANES), F32)] * 4
                       + [pltpu.VMEM((2 * tq, 2 * V_DIM), F32)],
        compiler_params=_params(2),
        name="diff_attention",
    )(q, k, v, kc, vc, diff_lambda, subln_g.reshape(1, V_DIM))


def _pool_kernel(p_ref, prev_ref, next_ref, w_ref, sc_ref, o_ref, ext, *, tm, n_tokens, gdim):
    i = pl.program_id(0)
    last = pl.num_programs(0) - 1
    ext[0:POOL_HALO, :] = jnp.where(i > 0, prev_ref[...], 0.0)
    ext[POOL_HALO:POOL_HALO + tm, :] = p_ref[...]
    ext[POOL_HALO + tm:2 * POOL_HALO + tm, :] = jnp.where(i < last, next_ref[...], 0.0)
    t = i * tm + lax.broadcasted_iota(I32, (tm, 1), 0)
    for g, w in enumerate(POOL_WINDOWS):
        cols = slice(g * gdim, (g + 1) * gdim)
        ssum = None
        for d in range(-(w // 2), w // 2):
            part = ext[POOL_HALO + d:POOL_HALO + d + tm, cols]
            ssum = part if ssum is None else ssum + part
        cnt = jnp.minimum(t + w // 2, n_tokens) - jnp.maximum(t - w // 2, 0)
        mean = ssum / cnt.astype(F32)
        mg = (mean - p_ref[:, cols]).astype(BF16)
        y = jnp.dot(mg, w_ref[g].astype(BF16), preferred_element_type=F32)
        o_ref[:, cols] = (y * sc_ref[:, cols]).astype(o_ref.dtype)


def _pool(p, w_pool, pool_scale):
    s_len, width = p.shape
    n_groups, gdim, _ = w_pool.shape
    assert n_groups == len(POOL_WINDOWS) and max(POOL_WINDOWS) // 2 <= POOL_HALO
    tm = _pick(s_len, (512, 256, 128))
    hb = tm // POOL_HALO
    n_halo_blocks = s_len // POOL_HALO
    return pl.pallas_call(
        functools.partial(_pool_kernel, tm=tm, n_tokens=s_len, gdim=gdim),
        grid=(s_len // tm,),
        in_specs=[pl.BlockSpec((tm, width), lambda i: (i, 0)),
                  pl.BlockSpec((POOL_HALO, width), lambda i: (jnp.maximum(i * hb - 1, 0), 0)),
                  pl.BlockSpec((POOL_HALO, width), lambda i: (jnp.minimum((i + 1) * hb, n_halo_blocks - 1), 0)),
                  pl.BlockSpec((n_groups, gdim, gdim), lambda i: (0, 0, 0)),
                  pl.BlockSpec((1, width), lambda i: (0, 0))],
        out_specs=pl.BlockSpec((tm, width), lambda i: (i, 0)),
        out_shape=jax.ShapeDtypeStruct((s_len, width), BF16),
        scratch_shapes=[pltpu.VMEM((tm + 2 * POOL_HALO, width), F32)],
        compiler_params=_params(1),
        name="pool_mixer",
    )(p, p, p, w_pool, pool_scale.reshape(1, width))


def _merge_kernel(a_ref, p_ref, wa_ref, wb_ref, ga_ref, gb_ref, o_ref, wabf, wbbf):
    @pl.when(pl.program_id(1) == 0)
    def _():
        wabf[...] = wa_ref[...].astype(BF16)
        wbbf[...] = wb_ref[...].astype(BF16)

    ya = jnp.dot(a_ref[...], wabf[...], preferred_element_type=F32)
    yb = jnp.dot(p_ref[...], wbbf[...], preferred_element_type=F32)
    o_ref[...] = (jax.nn.sigmoid(ga_ref[...]) * ya + jax.nn.sigmoid(gb_ref[...]) * yb).astype(o_ref.dtype)


def _merge(attn, pooled, w_a, w_b, gates):
    m, ka = attn.shape
    kb = pooled.shape[1]
    n = w_a.shape[1]
    tn = _pick(n, (512, 256, 128))
    tm = _pick(m, (1024, 512, 256, 128))
    gb_off = n // tn
    return pl.pallas_call(
        _merge_kernel,
        grid=(n // tn, m // tm),
        in_specs=[pl.BlockSpec((tm, ka), lambda j, i: (i, 0)),
                  pl.BlockSpec((tm, kb), lambda j, i: (i, 0)),
                  pl.BlockSpec((ka, tn), lambda j, i: (0, j)),
                  pl.BlockSpec((kb, tn), lambda j, i: (0, j)),
                  pl.BlockSpec((tm, tn), lambda j, i: (i, j)),
                  pl.BlockSpec((tm, tn), lambda j, i: (i, j + gb_off))],
        out_specs=pl.BlockSpec((tm, tn), lambda j, i: (i, j)),
        out_shape=jax.ShapeDtypeStruct((m, n), BF16),
        scratch_shapes=[pltpu.VMEM((ka, tn), BF16), pltpu.VMEM((kb, tn), BF16)],
        compiler_params=_params(2),
        name="gated_merge",
    )(attn, pooled, w_a, w_b, gates, gates)


def _post1_kernel(x_ref, y_ref, g1_ref, sh_ref, sc_ref, lg_ref, lb_ref, wr_ref, x1_ref, hp_ref, aff_ref, *, half):
    z = DEEPNORM_ALPHA * x_ref[...] + g1_ref[0:1, :] * y_ref[...]
    x1 = _ln_rows(z, LN_EPS) * lg_ref[...] + lb_ref[...]
    x1_ref[...] = x1
    h2 = _ln_rows(x1, ADALN_EPS) * (1.0 + sc_ref[0:1, :]) + sh_ref[0:1, :]
    hb = lax.bitcast_convert_type(h2.astype(BF16).astype(F32), U32)
    hp_ref[...] = (hb[:, :half] >> 16) | (hb[:, half:] & jnp.uint32(0xFFFF0000))
    logits = lax.dot_general(wr_ref[...], h2, (((1,), (1,)), ((), ())),
                             precision=lax.Precision.HIGHEST, preferred_element_type=F32)
    mx = jnp.max(logits, axis=0, keepdims=True)
    ex = jnp.exp(logits - mx)
    aff_ref[...] = ex / jnp.sum(ex, axis=0, keepdims=True)


def _post1(x2d, y, mods, ln_g, ln_b, w_router_t):
    m, d = x2d.shape
    n_exp = w_router_t.shape[0]
    tm = _pick(m, (256, 128))
    row = lambda i: (i, 0)
    vec = lambda i: (0, 0)
    return pl.pallas_call(
        functools.partial(_post1_kernel, half=d // 2),
        grid=(m // tm,),
        in_specs=[pl.BlockSpec((tm, d), row), pl.BlockSpec((tm, d), row),
                  pl.BlockSpec((SUBLANES, d), lambda i: (0, 2)),
                  pl.BlockSpec((SUBLANES, d), lambda i: (0, 3)),
                  pl.BlockSpec((SUBLANES, d), lambda i: (0, 4)),
                  pl.BlockSpec((1, d), vec), pl.BlockSpec((1, d), vec),
                  pl.BlockSpec((n_exp, d), vec)],
        out_specs=[pl.BlockSpec((tm, d), row), pl.BlockSpec((tm, d // 2), row),
                   pl.BlockSpec((n_exp, tm), lambda i: (0, i))],
        out_shape=[jax.ShapeDtypeStruct((m, d), F32), jax.ShapeDtypeStruct((m, d // 2), U32),
                   jax.ShapeDtypeStruct((n_exp, m), F32)],
        compiler_params=_params(1),
        name="ln1_router",
    )(x2d, y, mods, mods, mods, ln_g.reshape(1, d), ln_b.reshape(1, d), w_router_t)


def _route_kernel(aff_ref, idx_ref, csel_sc, *, n_exp, nc, cap, sblk):
    rows = n_exp * nc
    aff3 = aff_ref[...]
    ri = lax.broadcasted_iota(I32, (rows, rows), 0)
    rj = lax.broadcasted_iota(I32, (rows, rows), 1)
    before = jnp.where(((ri // nc) == (rj // nc)) & (rj < ri), 1.0, 0.0).astype(BF16)
    li = lax.broadcasted_iota(I32, (LANES, LANES), 0)
    lj = lax.broadcasted_iota(I32, (LANES, LANES), 1)
    tri = jnp.where(li <= lj, 1.0, 0.0).astype(BF16)
    ones = jnp.ones((LANES, LANES), BF16)

    def per_expert_sum(x3):
        return jnp.sum(jnp.sum(x3, axis=1, keepdims=True), axis=2, keepdims=True)

    def count(mask3):
        return per_expert_sum(jnp.where(mask3, 1.0, 0.0))

    def prefix(mask2):
        mb = jnp.where(mask2, 1.0, 0.0).astype(BF16)
        within = jnp.dot(mb, tri, preferred_element_type=F32)
        row_tot = jnp.dot(mb, ones, preferred_element_type=F32).astype(BF16)
        return within + jnp.dot(before, row_tot, preferred_element_type=F32)

    def rows_of(x3):
        return jnp.broadcast_to(x3, (n_exp, nc, LANES)).reshape(rows, LANES)

    top = jnp.max(jnp.max(aff3, axis=1, keepdims=True), axis=2, keepdims=True)
    lo0 = jnp.zeros_like(top)
    hi0 = 2.0 * top + 1e-30

    def bisect(_, carry):
        lo, hi = carry
        mid = 0.5 * (lo + hi)
        ok = count(aff3 >= mid) >= cap
        return jnp.where(ok, mid, lo), jnp.where(ok, hi, mid)

    thr3, _ = lax.fori_loop(0, BISECT_STEPS, bisect, (lo0, hi0))
    need = rows_of(cap - count(aff3 > thr3))
    aff2 = aff3.reshape(rows, LANES)
    thr2 = rows_of(thr3)
    gt = aff2 > thr2
    eq = aff2 == thr2
    sel = gt | (eq & (prefix(eq) <= need))
    csel_sc[...] = jnp.where(sel, prefix(sel), 0.0)

    lane_i = lax.broadcasted_iota(I32, (1, LANES), 1)
    slot0 = lax.broadcasted_iota(I32, (sblk, LANES), 0).astype(F32) + 1.0

    def per_expert(e, carry):
        for sb in range(cap // sblk):
            slot = slot0 + float(sb * sblk)

            def per_chunk(c, acc):
                hit = csel_sc[pl.ds(e * nc + c, 1), :] == slot
                return acc + jnp.where(hit, (lane_i + c * LANES).astype(F32), 0.0)

            acc = lax.fori_loop(0, nc, per_chunk, jnp.zeros((sblk, LANES), F32))
            idx_ref[e, sb * sblk:(sb + 1) * sblk, :] = jnp.sum(acc, axis=-1, keepdims=True).astype(I32)
        return carry

    lax.fori_loop(0, n_exp, per_expert, 0)


def _route(aff_t, cap):
    n_exp, s_len = aff_t.shape
    nc = s_len // LANES
    rows = n_exp * nc
    sblk = _pick(cap, (256, 128))
    return pl.pallas_call(
        functools.partial(_route_kernel, n_exp=n_exp, nc=nc, cap=cap, sblk=sblk),
        grid=(1,),
        in_specs=[pl.BlockSpec((n_exp, nc, LANES), lambda i: (0, 0, 0))],
        out_specs=pl.BlockSpec((n_exp, cap, 1), lambda i: (0, 0, 0)),
        out_shape=jax.ShapeDtypeStruct((n_exp, cap, 1), I32),
        scratch_shapes=[pltpu.VMEM((rows, LANES), F32)],
        compiler_params=_params(1),
        name="expert_choice",
    )(aff_t.reshape(n_exp, nc, LANES))


def _ffn_kernel(idx_ref, hp_hbm, wg_ref, wu_ref, wd_ref, o_ref, hsel, hb, act, sem,
                *, n_exp, cap, nf, nn, tf, half):
    e = pl.program_id(0)
    s = pl.program_id(1)
    n_steps = nf + nn
    rows_per_step = cap // nn
    nxt = jnp.minimum(e + 1, n_exp - 1)

    def start_row(ee, r):
        t = idx_ref[ee * cap + r]
        pltpu.make_async_copy(hp_hbm.at[pl.ds(t, 1), :], hsel.at[pl.ds(r, 1), :], sem.at[0]).start()

    def wait_gather():
        pltpu.make_async_copy(hp_hbm.at[pl.ds(0, cap), :], hsel, sem.at[0]).wait()

    def gather_slice():
        for r in range(rows_per_step):
            start_row(nxt, (s - nf) * rows_per_step + r)

    @pl.when((e == 0) & (s == 0))
    def _():
        def go(r, carry):
            start_row(0, r)
            return carry
        lax.fori_loop(0, cap, go, 0, unroll=GATHER_UNROLL)

    @pl.when(s == 0)
    def _():
        wait_gather()
        u = hsel[...]
        hb[:, :half] = lax.bitcast_convert_type(u << 16, F32).astype(BF16)
        hb[:, half:] = lax.bitcast_convert_type(u & jnp.uint32(0xFFFF0000), F32).astype(BF16)

    @pl.when(s < nf)
    def _():
        x = hb[...]
        u = jnp.dot(x, wg_ref[0].astype(BF16), preferred_element_type=F32)
        v = jnp.dot(x, wu_ref[0].astype(BF16), preferred_element_type=F32)
        act[s] = (u * jax.nn.sigmoid(u) * v).astype(BF16)

    @pl.when(s >= nf)
    def _():
        gather_slice()
        o = jnp.dot(act[0], wd_ref[0, 0:tf, :].astype(BF16), preferred_element_type=F32)
        for f in range(1, nf):
            o += jnp.dot(act[f], wd_ref[0, f * tf:(f + 1) * tf, :].astype(BF16), preferred_element_type=F32)
        o_ref[...] = o.astype(o_ref.dtype)

    @pl.when((e == n_exp - 1) & (s == n_steps - 1))
    def _():
        wait_gather()


def _expert_ffn(idx_flat, hp, w_gate, w_up, w_down):
    n_exp, d, ff = w_gate.shape
    cap = idx_flat.shape[0] // n_exp
    half = hp.shape[1]
    tf = _pick(ff, (256, 128))
    tn = _pick(d, (512, 256, 128))
    nf, nn = ff // tf, d // tn
    assert cap % nn == 0
    last_e = n_exp - 1

    def up_map(e, s, idx):
        return (jnp.where(s < nf, e, jnp.minimum(e + 1, last_e)), 0, jnp.where(s < nf, s, 0))

    def down_map(e, s, idx):
        return (e, 0, jnp.maximum(s - nf, 0))

    def out_map(e, s, idx):
        return (e, jnp.maximum(s - nf, 0))

    grid_spec = pltpu.PrefetchScalarGridSpec(
        num_scalar_prefetch=1,
        grid=(n_exp, nf + nn),
        in_specs=[pl.BlockSpec(memory_space=pl.ANY),
                  pl.BlockSpec((1, d, tf), up_map),
                  pl.BlockSpec((1, d, tf), up_map),
                  pl.BlockSpec((1, ff, tn), down_map)],
        out_specs=pl.BlockSpec((cap, tn), out_map),
        scratch_shapes=[pltpu.VMEM((cap, half), U32), pltpu.VMEM((cap, d), BF16),
                        pltpu.VMEM((nf, cap, tf), BF16), pltpu.SemaphoreType.DMA((1,))],
    )
    return pl.pallas_call(
        functools.partial(_ffn_kernel, n_exp=n_exp, cap=cap, nf=nf, nn=nn, tf=tf, half=half),
        grid_spec=grid_spec,
        out_shape=jax.ShapeDtypeStruct((n_exp * cap, d), BF16),
        compiler_params=_params(2),
        name="expert_ffn",
    )(idx_flat, hp, w_gate, w_up, w_down)


def _combine_kernel(ib_ref, ic_ref, fl_ref, o_ref, idx_ref, aff_ref, x1_ref, g2_ref, lg_ref, lb_ref, out_ref, acc,
                    *, tb, ck, nch):
    it = pl.program_id(0)
    flags = fl_ref[it]

    @pl.when((flags & 2) != 0)
    def _():
        acc[...] = jnp.zeros_like(acc)

    @pl.when((flags & 1) != 0)
    def _():
        tok = ib_ref[it] * tb + lax.broadcasted_iota(I32, (tb, ck), 0)
        onehot = jnp.where(tok == idx_ref[0], 1.0, 0.0).astype(BF16)
        aff = aff_ref[...]
        expert = ic_ref[it] // nch
        gate = jnp.sum(jnp.where(lax.broadcasted_iota(I32, aff.shape, 1) == expert, aff, 0.0),
                       axis=-1, keepdims=True)
        acc[...] += gate * jnp.dot(onehot, o_ref[...], preferred_element_type=F32)

    @pl.when((flags & 4) != 0)
    def _():
        z = DEEPNORM_ALPHA * x1_ref[...] + g2_ref[0:1, :] * acc[...]
        out_ref[...] = _ln_rows(z, LN_EPS) * lg_ref[...] + lb_ref[...]


def _combine(expert_out, idx, aff, x1, mods, ln_g, ln_b):
    n_exp, cap = idx.shape
    s_len, d = x1.shape
    tb = _pick(s_len, (512, 256, 128))
    ck = _pick(cap, (256, 128))
    nb, nch = s_len // tb, cap // ck
    n_items = n_exp * nch + n_exp * nb

    idx3 = idx.reshape(n_exp, nch, ck)
    lo = idx3[:, :, 0] // tb
    hi = idx3[:, :, ck - 1] // tb
    bb = jnp.arange(nb, dtype=I32)[:, None, None]
    hit = (lo[None] <= bb) & (bb <= hi[None])
    empty = ~jnp.any(hit, axis=(1, 2))
    hit = hit.at[:, 0, 0].set(hit[:, 0, 0] | empty)
    flat = hit.reshape(-1)
    (ids,) = jnp.nonzero(flat, size=n_items, fill_value=0)
    count = jnp.sum(flat.astype(I32))
    pos = jnp.arange(n_items, dtype=I32)
    valid = pos < count
    ids = jnp.where(valid, ids, ids[count - 1]).astype(I32)
    item_b = ids // (n_exp * nch)
    item_c = ids % (n_exp * nch)
    prev_b = jnp.concatenate([jnp.full((1,), -1, I32), item_b[:-1]])
    next_b = jnp.concatenate([item_b[1:], jnp.full((1,), -1, I32)])
    next_valid = jnp.concatenate([valid[1:], jnp.zeros((1,), bool)])
    first = valid & (item_b != prev_b)
    last = valid & ((item_b != next_b) | ~next_valid)
    flags = valid.astype(I32) + 2 * first.astype(I32) + 4 * last.astype(I32)

    grid_spec = pltpu.PrefetchScalarGridSpec(
        num_scalar_prefetch=3,
        grid=(n_items,),
        in_specs=[pl.BlockSpec((ck, d), lambda it, ib, ic, fl: (ic[it], 0)),
                  pl.BlockSpec((1, 1, ck), lambda it, ib, ic, fl: (ic[it], 0, 0)),
                  pl.BlockSpec((tb, n_exp), lambda it, ib, ic, fl: (ib[it], 0)),
                  pl.BlockSpec((tb, d), lambda it, ib, ic, fl: (ib[it], 0)),
                  pl.BlockSpec((SUBLANES, d), lambda it, ib, ic, fl: (0, 5)),
                  pl.BlockSpec((1, d), lambda it, ib, ic, fl: (0, 0)),
                  pl.BlockSpec((1, d), lambda it, ib, ic, fl: (0, 0))],
        out_specs=pl.BlockSpec((tb, d), lambda it, ib, ic, fl: (ib[it], 0)),
        scratch_shapes=[pltpu.VMEM((tb, d), F32)],
    )
    return pl.pallas_call(
        functools.partial(_combine_kernel, tb=tb, ck=ck, nch=nch),
        grid_spec=grid_spec,
        out_shape=jax.ShapeDtypeStruct((s_len, d), F32),
        compiler_params=_params(1),
        name="combine_ln2",
    )(item_b, item_c, flags, expert_out, idx.reshape(n_exp * nch, 1, ck), aff, x1, mods,
      ln_g.reshape(1, d), ln_b.reshape(1, d))


def kernel(x, c, ctx, c_ctx, w_mod, b_mod, w_in, diff_lambda, diff_subln_g, w_pool, pool_scale, w_branch_attn,
           w_branch_pool, w_out, ln1_g, ln1_b, w_router, w_gate, w_up, w_down, ln2_g, ln2_b):
    batch, s_len, d = x.shape
    assert batch == 1 and w_mod.shape[0] == DEPTH
    attn_w = N_HEADS * V_DIM
    pool_w = w_pool.shape[1] * w_pool.shape[2]
    k_off, v_off, p_off = attn_w, 2 * attn_w, 3 * attn_w
    g_off = p_off + pool_w
    n_exp = w_router.shape[-1]
    cap = EC_CAPACITY_FACTOR * s_len // n_exp

    mods = _modulation(c, c_ctx, w_mod[0], b_mod[0])
    h = _premod(x[0], mods, 0)
    hc = _premod(ctx[0], mods, 1)

    w = w_in[0]
    tabs = _rope_tables(s_len)
    q = _project(h, w, 0, attn_w, BF16, tabs, HEAD_DIM ** -0.5 * math.log2(math.e))
    k = _project(h, w, k_off, attn_w, BF16, tabs)
    v = _project(h, w, v_off, attn_w, BF16)
    kc = _project(hc, w, k_off, attn_w, BF16)
    vc = _project(hc, w, v_off, attn_w, BF16)
    p = _project(h, w, p_off, pool_w, F32)
    gates = _project(h, w, g_off, 2 * d, F32)

    attn = _attention(q, k, v, kc, vc, diff_lambda[0], diff_subln_g[0])
    pooled = _pool(p, w_pool[0], pool_scale[0])
    merged = _merge(attn, pooled, w_branch_attn[0], w_branch_pool[0], gates)
    y = _project(merged, w_out[0], 0, d, F32)

    x1, hp, aff_t = _post1(x[0], y, mods, ln1_g[0], ln1_b[0], w_router[0].T)
    idx = _route(aff_t, cap).reshape(n_exp, cap)
    expert_out = _expert_ffn(idx.reshape(-1), hp, w_gate[0], w_up[0], w_down[0])
    out = _combine(expert_out, idx, aff_t.T, x1, mods, ln2_g[0], ln2_b[0])
    return out.reshape(batch, s_len, d)
```

```python
import functools
import math

import jax
import jax.numpy as jnp
from jax import lax
from jax.experimental import pallas as pl
from jax.experimental.pallas import tpu as pltpu

F32 = jnp.float32
BF16 = jnp.bfloat16
I32 = jnp.int32
U32 = jnp.uint32

LANES = 128
SUBLANES = 8
VMEM_LIMIT = 56 * 1024 * 1024

GRID_W = 64
N_HEADS = 16
HEAD_DIM = 64
V_DIM = 2 * HEAD_DIM
POOL_WINDOWS = (2, 4, 8, 16)
POOL_HALO = 8
EC_CAPACITY_FACTOR = 2
ROPE_BASE = 10000.0
LN_EPS = 1e-5
ADALN_EPS = 1e-6
SUBLN_EPS = 1e-5
DEPTH = 1
DEEPNORM_ALPHA = (2.0 * DEPTH) ** 0.25
LAMBDA_INIT = 0.8 - 0.6 * math.exp(-0.3 * 0)
NEG_BIG = -1e30
BISECT_STEPS = 64
GATHER_UNROLL = 8
ROPE_ROW_PARTS = 8
MERGE_ROW_PARTS = 4
PIECE = 16
COMBINE_GROUP = 256
COMBINE_ROWS = 1024
LN_STRIP = 16
ATTN_STRIP = 32


def _pick(n, prefs):
    for p in prefs:
        if n % p == 0:
            return p
    raise ValueError(f"no tile for {n} in {prefs}")


def _params(n_axes):
    return pltpu.CompilerParams(dimension_semantics=("arbitrary",) * n_axes,
                                vmem_limit_bytes=VMEM_LIMIT)


def _ln_rows(x, eps):
    mu = jnp.mean(x, axis=-1, keepdims=True)
    xc = x - mu
    var = jnp.mean(xc * xc, axis=-1, keepdims=True)
    return xc * lax.rsqrt(var + eps)


def _mod_kernel(cb_ref, w_ref, b_ref, o_ref, *, tn):
    o_ref[...] = jnp.zeros_like(o_ref)
    for i in range(2):
        cv = cb_ref[i]
        s = jnp.tile(cv * jax.nn.sigmoid(cv), (1, tn // LANES))
        o_ref[i:i + 1, :] = jnp.sum(w_ref[...] * s, axis=0, keepdims=True) + b_ref[...]


def _modulation(c, c_ctx, w_mod, b_mod):
    d, n = w_mod.shape
    tn = _pick(n, (512, 256, 128))
    cb = jnp.broadcast_to(jnp.stack([c[0], c_ctx])[:, :, None], (2, d, LANES))
    return pl.pallas_call(
        functools.partial(_mod_kernel, tn=tn),
        grid=(n // tn,),
        in_specs=[pl.BlockSpec((2, d, LANES), lambda j: (0, 0, 0)),
                  pl.BlockSpec((d, tn), lambda j: (0, j)),
                  pl.BlockSpec((1, tn), lambda j: (0, j))],
        out_specs=pl.BlockSpec((SUBLANES, tn), lambda j: (0, j)),
        out_shape=jax.ShapeDtypeStruct((SUBLANES, n), F32),
        compiler_params=_params(1),
        name="modulation",
    )(cb, w_mod, b_mod.reshape(1, n))


def _premod_kernel(x_ref, sh_ref, sc_ref, o_ref, *, row):
    scale = 1.0 + sc_ref[row:row + 1, :]
    shift = sh_ref[row:row + 1, :]
    for r0 in range(0, x_ref.shape[0], LN_STRIP):
        rows = slice(r0, r0 + LN_STRIP)
        o_ref[rows, :] = (_ln_rows(x_ref[rows, :], ADALN_EPS) * scale + shift).astype(BF16)


def _premod(x2d, mods, row):
    m, d = x2d.shape
    tm = _pick(m, (256, 128))
    return pl.pallas_call(
        functools.partial(_premod_kernel, row=row),
        grid=(m // tm,),
        in_specs=[pl.BlockSpec((tm, d), lambda i: (i, 0)),
                  pl.BlockSpec((SUBLANES, d), lambda i: (0, 0)),
                  pl.BlockSpec((SUBLANES, d), lambda i: (0, 1))],
        out_specs=pl.BlockSpec((tm, d), lambda i: (i, 0)),
        out_shape=jax.ShapeDtypeStruct((m, d), BF16),
        compiler_params=_params(1),
        name="premod",
    )(x2d, mods, mods)


def _proj_kernel(*refs, rope, scale, tn):
    if rope:
        h_ref, w_ref, cos_ref, sa_ref, sb_ref, o_ref, wbf = refs
    else:
        h_ref, w_ref, o_ref, wbf = refs

    @pl.when(pl.program_id(1) == 0)
    def _():
        wbf[...] = w_ref[...].astype(BF16)

    if rope:
        half = h_ref.shape[0] // ROPE_ROW_PARTS
        for r0 in range(0, h_ref.shape[0], half):
            rows = slice(r0, r0 + half)
            acc = jnp.dot(h_ref[rows, :], wbf[...], preferred_element_type=F32)
            cos, sa, sb = cos_ref[rows, :], sa_ref[rows, :], sb_ref[rows, :]
            for c in range(tn // LANES):
                cols = slice(c * LANES, (c + 1) * LANES)
                a = acc[:, cols]
                r = a * cos + pltpu.roll(a, LANES - 16, 1) * sa + pltpu.roll(a, 16, 1) * sb
                o_ref[rows, cols] = (r * scale).astype(o_ref.dtype)
    else:
        o_ref[...] = jnp.dot(h_ref[...], wbf[...], preferred_element_type=F32).astype(o_ref.dtype)


def _project(h, w, col_off, n_cols, out_dtype, rope_tabs=None, scale=1.0):
    m, k = h.shape
    tn = _pick(math.gcd(n_cols, col_off), (512, 256, 128))
    tm = _pick(m, (1024, 512, 256, 128))
    off = col_off // tn
    in_specs = [pl.BlockSpec((tm, k), lambda j, i: (i, 0)),
                pl.BlockSpec((k, tn), lambda j, i: (0, j + off))]
    args = [h, w]
    if rope_tabs is not None:
        in_specs += [pl.BlockSpec((tm, LANES), lambda j, i: (i, 0))] * 3
        args += list(rope_tabs)
    return pl.pallas_call(
        functools.partial(_proj_kernel, rope=rope_tabs is not None, scale=scale, tn=tn),
        grid=(n_cols // tn, m // tm),
        in_specs=in_specs,
        out_specs=pl.BlockSpec((tm, tn), lambda j, i: (i, j)),
        out_shape=jax.ShapeDtypeStruct((m, n_cols), out_dtype),
        scratch_shapes=[pltpu.VMEM((k, tn), BF16)],
        compiler_params=_params(2),
        name="project",
    )(*args)


def _rope_tables(n_tokens):
    rows = n_tokens // GRID_W
    row = jnp.broadcast_to(jnp.arange(rows, dtype=F32)[:, None], (rows, GRID_W)).reshape(-1)
    col = jnp.broadcast_to(jnp.arange(GRID_W, dtype=F32)[None, :], (rows, GRID_W)).reshape(-1)
    axis_dim = HEAD_DIM // 2
    nf = axis_dim // 2
    inv_freq = ROPE_BASE ** (-jnp.arange(0, axis_dim, 2, dtype=F32) / axis_dim)
    ang = jnp.stack([row[:, None] * inv_freq, col[:, None] * inv_freq], axis=1)
    cos, sin = jnp.cos(ang), jnp.sin(ang)
    zero = jnp.zeros_like(sin)
    cos64 = jnp.concatenate([cos, cos], axis=-1).reshape(n_tokens, HEAD_DIM)
    sa64 = jnp.concatenate([-sin, zero], axis=-1).reshape(n_tokens, HEAD_DIM)
    sb64 = jnp.concatenate([zero, sin], axis=-1).reshape(n_tokens, HEAD_DIM)
    rep = LANES // HEAD_DIM
    return tuple(jnp.tile(t, (1, rep)) for t in (cos64, sa64, sb64))


def _attn_kernel(q_ref, k_ref, v_ref, kc_ref, vc_ref, dl_ref, g_ref, o_ref, qs_sc, vx_sc, sa_sc, sb_sc, sc_sc,
                 pa_sc, pb_sc, xa_sc, xb_sc, xc_sc, m_sc, acc_sc, *, tq, tk, n_chunks):
    s_len = n_chunks * tk
    c_len = kc_ref.shape[0]

    @pl.when(pl.program_id(1) == 0)
    def _():
        vx_sc[0:s_len, 0:V_DIM] = v_ref[...]
        vx_sc[s_len:s_len + c_len, 0:V_DIM] = vc_ref[...]
        vx_sc[:, V_DIM:2 * V_DIM] = jnp.ones((s_len + c_len, V_DIM), BF16)

    q = q_ref[...]
    lane = lax.broadcasted_iota(I32, q.shape, 1)
    zero = jnp.zeros_like(q)
    qs_sc[0:tq, :] = jnp.where(lane < HEAD_DIM, q, zero)
    qs_sc[tq:2 * tq, :] = jnp.where(lane >= HEAD_DIM, q, zero)
    m_sc[...] = jnp.full_like(m_sc, NEG_BIG)
    acc_sc[...] = jnp.zeros_like(acc_sc)

    def scores(s_ref, x_ref, kb):
        s = lax.dot_general(qs_sc[...], kb, (((1,), (1,)), ((), ())), preferred_element_type=F32)
        s_ref[...] = s
        x_ref[...] = jnp.broadcast_to(jnp.max(s, axis=-1, keepdims=True), x_ref.shape)

    def absorb(s_ref, x_ref, p_ref, v0):
        width = s_ref.shape[1]
        for r0 in range(0, 2 * tq, ATTN_STRIP):
            rows = slice(r0, r0 + ATTN_STRIP)
            m_prev = m_sc[rows, :]
            m_new = jnp.maximum(m_prev, x_ref[rows, :])
            x_ref[rows, :] = jnp.exp2(m_prev - m_new)
            m_sc[rows, :] = m_new
            p_ref[rows, 0:width] = jnp.exp2((s_ref[rows, :] - jnp.tile(m_new, (1, width // LANES))).astype(BF16))
        pv = jnp.dot(p_ref[:, 0:width], vx_sc[pl.ds(v0, width), :], preferred_element_type=F32)
        acc_sc[...] = jnp.tile(x_ref[...], (1, 2)) * acc_sc[...] + pv

    def kchunk(c):
        return k_ref[pl.ds(pl.multiple_of(c * tk, tk), tk), :]

    scores(sa_sc, xa_sc, kchunk(0))

    def body(j, carry):
        c = 2 * j
        scores(sb_sc, xb_sc, kchunk(c + 1))
        absorb(sa_sc, xa_sc, pa_sc, pl.multiple_of(c * tk, tk))
        scores(sa_sc, xa_sc, kchunk(c + 2))
        absorb(sb_sc, xb_sc, pb_sc, pl.multiple_of((c + 1) * tk, tk))
        return carry

    lax.fori_loop(0, (n_chunks - 2) // 2, body, 0)
    scores(sb_sc, xb_sc, kchunk(n_chunks - 1))
    absorb(sa_sc, xa_sc, pa_sc, (n_chunks - 2) * tk)
    scores(sc_sc, xc_sc, kc_ref[...])
    absorb(sb_sc, xb_sc, pb_sc, (n_chunks - 1) * tk)
    absorb(sc_sc, xc_sc, pa_sc, s_len)

    dl = dl_ref[...]
    lam = (jnp.exp(jnp.sum(dl[0:1] * dl[1:2], axis=-1, keepdims=True))
           - jnp.exp(jnp.sum(dl[2:3] * dl[3:4], axis=-1, keepdims=True)) + LAMBDA_INIT)
    o = (acc_sc[0:tq, 0:V_DIM] / acc_sc[0:tq, V_DIM:2 * V_DIM]
         - lam * (acc_sc[tq:2 * tq, 0:V_DIM] / acc_sc[tq:2 * tq, V_DIM:2 * V_DIM]))
    o = o * lax.rsqrt(jnp.mean(o * o, axis=-1, keepdims=True) + SUBLN_EPS)
    o_ref[...] = (o * g_ref[...] * (1.0 - LAMBDA_INIT)).astype(o_ref.dtype)


def _attention(q, k, v, kc, vc, diff_lambda, subln_g):
    s_len, width = q.shape
    c_len = kc.shape[0]
    tq = _pick(s_len, (512, 256, 128))
    tk = _pick(s_len // 2, (2048, 1024, 512, 256, 128))
    n_chunks = s_len // tk
    assert n_chunks % 2 == 0
    n_heads = width // V_DIM
    return pl.pallas_call(
        functools.partial(_attn_kernel, tq=tq, tk=tk, n_chunks=n_chunks),
        grid=(n_heads, s_len // tq),
        in_specs=[pl.BlockSpec((tq, V_DIM), lambda h, i: (i, h)),
                  pl.BlockSpec((s_len, V_DIM), lambda h, i: (0, h)),
                  pl.BlockSpec((s_len, V_DIM), lambda h, i: (0, h)),
                  pl.BlockSpec((c_len, V_DIM), lambda h, i: (0, h)),
                  pl.BlockSpec((c_len, V_DIM), lambda h, i: (0, h)),
                  pl.BlockSpec((4, HEAD_DIM), lambda h, i: (0, 0)),
                  pl.BlockSpec((1, V_DIM), lambda h, i: (0, 0))],
        out_specs=pl.BlockSpec((tq, V_DIM), lambda h, i: (i, h)),
        out_shape=jax.ShapeDtypeStruct((s_len, width), BF16),
        scratch_shapes=[pltpu.VMEM((2 * tq, V_DIM), BF16), pltpu.VMEM((s_len + c_len, 2 * V_DIM), BF16),
                        pltpu.VMEM((2 * tq, tk), F32), pltpu.VMEM((2 * tq, tk), F32),
                        pltpu.VMEM((2 * tq, c_len), F32)]
                       + [pltpu.VMEM((2 * tq, tk), BF16)] * 2
                       + [pltpu.VMEM((2 * tq, LANES), F32)] * 4
                       + [pltpu.VMEM((2 * tq, 2 * V_DIM), F32)],
        compiler_params=_params(2),
        name="diff_attention",
    )(q, k, v, kc, vc, diff_lambda, subln_g.reshape(1, V_DIM))


def _pool_kernel(p_ref, prev_ref, next_ref, w_ref, sc_ref, o_ref, ext, *, tm, n_tokens, gdim):
    i = pl.program_id(0)
    last = pl.num_programs(0) - 1
    ext[0:POOL_HALO, :] = jnp.where(i > 0, prev_ref[...], 0.0)
    ext[POOL_HALO:POOL_HALO + tm, :] = p_ref[...]
    ext[POOL_HALO + tm:2 * POOL_HALO + tm, :] = jnp.where(i < last, next_ref[...], 0.0)
    t = i * tm + lax.broadcasted_iota(I32, (tm, 1), 0)
    for g, w in enumerate(POOL_WINDOWS):
        cols = slice(g * gdim, (g + 1) * gdim)
        ssum = None
        for d in range(-(w // 2), w // 2):
            part = ext[POOL_HALO + d:POOL_HALO + d + tm, cols]
            ssum = part if ssum is None else ssum + part
        cnt = jnp.minimum(t + w // 2, n_tokens) - jnp.maximum(t - w // 2, 0)
        mean = ssum / cnt.astype(F32)
        mg = (mean - p_ref[:, cols]).astype(BF16)
        y = jnp.dot(mg, w_ref[g].astype(BF16), preferred_element_type=F32)
        o_ref[:, cols] = (y * sc_ref[:, cols]).astype(o_ref.dtype)


def _pool(p, w_pool, pool_scale):
    s_len, width = p.shape
    n_groups, gdim, _ = w_pool.shape
    assert n_groups == len(POOL_WINDOWS) and max(POOL_WINDOWS) // 2 <= POOL_HALO
    tm = _pick(s_len, (512, 256, 128))
    hb = tm // POOL_HALO
    n_halo_blocks = s_len // POOL_HALO
    return pl.pallas_call(
        functools.partial(_pool_kernel, tm=tm, n_tokens=s_len, gdim=gdim),
        grid=(s_len // tm,),
        in_specs=[pl.BlockSpec((tm, width), lambda i: (i, 0)),
                  pl.BlockSpec((POOL_HALO, width), lambda i: (jnp.maximum(i * hb - 1, 0), 0)),
                  pl.BlockSpec((POOL_HALO, width), lambda i: (jnp.minimum((i + 1) * hb, n_halo_blocks - 1), 0)),
                  pl.BlockSpec((n_groups, gdim, gdim), lambda i: (0, 0, 0)),
                  pl.BlockSpec((1, width), lambda i: (0, 0))],
        out_specs=pl.BlockSpec((tm, width), lambda i: (i, 0)),
        out_shape=jax.ShapeDtypeStruct((s_len, width), BF16),
        scratch_shapes=[pltpu.VMEM((tm + 2 * POOL_HALO, width), F32)],
        compiler_params=_params(1),
        name="pool_mixer",
    )(p, p, p, w_pool, pool_scale.reshape(1, width))


def _merge_kernel(a_ref, p_ref, wa_ref, wb_ref, ga_ref, gb_ref, o_ref, wabf, wbbf):
    @pl.when(pl.program_id(1) == 0)
    def _():
        wabf[...] = wa_ref[...].astype(BF16)
        wbbf[...] = wb_ref[...].astype(BF16)

    part = a_ref.shape[0] // MERGE_ROW_PARTS
    for r0 in range(0, a_ref.shape[0], part):
        rows = slice(r0, r0 + part)
        ya = jnp.dot(a_ref[rows, :], wabf[...], preferred_element_type=F32)
        yb = jnp.dot(p_ref[rows, :], wbbf[...], preferred_element_type=F32)
        o_ref[rows, :] = (jax.nn.sigmoid(ga_ref[rows, :]) * ya
                          + jax.nn.sigmoid(gb_ref[rows, :]) * yb).astype(o_ref.dtype)


def _merge(attn, pooled, w_a, w_b, gates):
    m, ka = attn.shape
    kb = pooled.shape[1]
    n = w_a.shape[1]
    tn = _pick(n, (512, 256, 128))
    tm = _pick(m, (1024, 512, 256, 128))
    gb_off = n // tn
    return pl.pallas_call(
        _merge_kernel,
        grid=(n // tn, m // tm),
        in_specs=[pl.BlockSpec((tm, ka), lambda j, i: (i, 0)),
                  pl.BlockSpec((tm, kb), lambda j, i: (i, 0)),
                  pl.BlockSpec((ka, tn), lambda j, i: (0, j)),
                  pl.BlockSpec((kb, tn), lambda j, i: (0, j)),
                  pl.BlockSpec((tm, tn), lambda j, i: (i, j)),
                  pl.BlockSpec((tm, tn), lambda j, i: (i, j + gb_off))],
        out_specs=pl.BlockSpec((tm, tn), lambda j, i: (i, j)),
        out_shape=jax.ShapeDtypeStruct((m, n), BF16),
        scratch_shapes=[pltpu.VMEM((ka, tn), BF16), pltpu.VMEM((kb, tn), BF16)],
        compiler_params=_params(2),
        name="gated_merge",
    )(attn, pooled, w_a, w_b, gates, gates)


def _post1_kernel(x_ref, y_ref, g1_ref, sh_ref, sc_ref, lg_ref, lb_ref, wr_ref, x1_ref, hp_ref, aff_ref, *, half):
    z = DEEPNORM_ALPHA * x_ref[...] + g1_ref[0:1, :] * y_ref[...]
    x1 = _ln_rows(z, LN_EPS) * lg_ref[...] + lb_ref[...]
    x1_ref[...] = x1
    h2 = _ln_rows(x1, ADALN_EPS) * (1.0 + sc_ref[0:1, :]) + sh_ref[0:1, :]
    hb = lax.bitcast_convert_type(h2.astype(BF16).astype(F32), U32)
    hp_ref[...] = (hb[:, :half] >> 16) | (hb[:, half:] & jnp.uint32(0xFFFF0000))
    logits = lax.dot_general(wr_ref[...], h2, (((1,), (1,)), ((), ())),
                             precision=lax.Precision.HIGHEST, preferred_element_type=F32)
    mx = jnp.max(logits, axis=0, keepdims=True)
    ex = jnp.exp(logits - mx)
    aff_ref[...] = ex / jnp.sum(ex, axis=0, keepdims=True)


def _post1(x2d, y, mods, ln_g, ln_b, w_router_t):
    m, d = x2d.shape
    n_exp = w_router_t.shape[0]
    tm = _pick(m, (256, 128))
    row = lambda i: (i, 0)
    vec = lambda i: (0, 0)
    return pl.pallas_call(
        functools.partial(_post1_kernel, half=d // 2),
        grid=(m // tm,),
        in_specs=[pl.BlockSpec((tm, d), row), pl.BlockSpec((tm, d), row),
                  pl.BlockSpec((SUBLANES, d), lambda i: (0, 2)),
                  pl.BlockSpec((SUBLANES, d), lambda i: (0, 3)),
                  pl.BlockSpec((SUBLANES, d), lambda i: (0, 4)),
                  pl.BlockSpec((1, d), vec), pl.BlockSpec((1, d), vec),
                  pl.BlockSpec((n_exp, d), vec)],
        out_specs=[pl.BlockSpec((tm, d), row), pl.BlockSpec((tm, d // 2), row),
                   pl.BlockSpec((n_exp, tm), lambda i: (0, i))],
        out_shape=[jax.ShapeDtypeStruct((m, d), F32), jax.ShapeDtypeStruct((m, d // 2), U32),
                   jax.ShapeDtypeStruct((n_exp, m), F32)],
        compiler_params=_params(1),
        name="ln1_router",
    )(x2d, y, mods, mods, mods, ln_g.reshape(1, d), ln_b.reshape(1, d), w_router_t)


def _route_kernel(aff_ref, idx_ref, csel_sc, *, n_exp, nc, cap, sblk):
    rows = n_exp * nc
    aff3 = aff_ref[...]
    ri = lax.broadcasted_iota(I32, (rows, rows), 0)
    rj = lax.broadcasted_iota(I32, (rows, rows), 1)
    before = jnp.where(((ri // nc) == (rj // nc)) & (rj < ri), 1.0, 0.0).astype(BF16)
    li = lax.broadcasted_iota(I32, (LANES, LANES), 0)
    lj = lax.broadcasted_iota(I32, (LANES, LANES), 1)
    tri = jnp.where(li <= lj, 1.0, 0.0).astype(BF16)
    ones = jnp.ones((LANES, LANES), BF16)

    def per_expert_sum(x3):
        return jnp.sum(jnp.sum(x3, axis=1, keepdims=True), axis=2, keepdims=True)

    def count(mask3):
        return per_expert_sum(jnp.where(mask3, 1.0, 0.0))

    def prefix(mask2):
        mb = jnp.where(mask2, 1.0, 0.0).astype(BF16)
        within = jnp.dot(mb, tri, preferred_element_type=F32)
        row_tot = jnp.dot(mb, ones, preferred_element_type=F32).astype(BF16)
        return within + jnp.dot(before, row_tot, preferred_element_type=F32)

    def rows_of(x3):
        return jnp.broadcast_to(x3, (n_exp, nc, LANES)).reshape(rows, LANES)

    top = jnp.max(jnp.max(aff3, axis=1, keepdims=True), axis=2, keepdims=True)
    lo0 = jnp.zeros_like(top)
    hi0 = 2.0 * top + 1e-30

    def bisect(_, carry):
        lo, hi = carry
        mid = 0.5 * (lo + hi)
        ok = count(aff3 >= mid) >= cap
        return jnp.where(ok, mid, lo), jnp.where(ok, hi, mid)

    thr3, _ = lax.fori_loop(0, BISECT_STEPS, bisect, (lo0, hi0))
    need = rows_of(cap - count(aff3 > thr3))
    aff2 = aff3.reshape(rows, LANES)
    thr2 = rows_of(thr3)
    gt = aff2 > thr2
    eq = aff2 == thr2
    sel = gt | (eq & (prefix(eq) <= need))
    csel_sc[...] = jnp.where(sel, prefix(sel), 0.0)

    lane_i = lax.broadcasted_iota(I32, (1, LANES), 1)
    slot0 = lax.broadcasted_iota(I32, (sblk, LANES), 0).astype(F32) + 1.0

    def per_expert(e, carry):
        for sb in range(cap // sblk):
            slot = slot0 + float(sb * sblk)

            def per_chunk(c, acc):
                hit = csel_sc[pl.ds(e * nc + c, 1), :] == slot
                return acc + jnp.where(hit, (lane_i + c * LANES).astype(F32), 0.0)

            acc = lax.fori_loop(0, nc, per_chunk, jnp.zeros((sblk, LANES), F32))
            idx_ref[e, sb * sblk:(sb + 1) * sblk, :] = jnp.sum(acc, axis=-1, keepdims=True).astype(I32)
        return carry

    lax.fori_loop(0, n_exp, per_expert, 0)


def _route(aff_t, cap):
    n_exp, s_len = aff_t.shape
    nc = s_len // LANES
    rows = n_exp * nc
    sblk = _pick(cap, (256, 128))
    return pl.pallas_call(
        functools.partial(_route_kernel, n_exp=n_exp, nc=nc, cap=cap, sblk=sblk),
        grid=(1,),
        in_specs=[pl.BlockSpec((n_exp, nc, LANES), lambda i: (0, 0, 0))],
        out_specs=pl.BlockSpec((n_exp, cap, 1), lambda i: (0, 0, 0)),
        out_shape=jax.ShapeDtypeStruct((n_exp, cap, 1), I32),
        scratch_shapes=[pltpu.VMEM((rows, LANES), F32)],
        compiler_params=_params(1),
        name="expert_choice",
    )(aff_t.reshape(n_exp, nc, LANES))


def _ffn_kernel(idx_ref, hp_hbm, aff_hbm, wg_ref, wu_ref, wd_ref, o_ref, hsel, gsel, hb, gate_sc, act, sem,
                *, n_exp, cap, nf, nn, tf, half):
    e = pl.program_id(0)
    s = pl.program_id(1)
    n_steps = nf + nn
    rows_per_step = cap // nn
    nxt = jnp.minimum(e + 1, n_exp - 1)

    def start_row(ee, r):
        t = idx_ref[ee * cap + r]
        pltpu.make_async_copy(hp_hbm.at[pl.ds(t, 1), :], hsel.at[pl.ds(r, 1), :], sem.at[0]).start()
        pltpu.make_async_copy(aff_hbm.at[pl.ds(t, 1), :], gsel.at[pl.ds(r, 1), :], sem.at[1]).start()

    def wait_gather():
        pltpu.make_async_copy(hp_hbm.at[pl.ds(0, cap), :], hsel, sem.at[0]).wait()
        pltpu.make_async_copy(aff_hbm.at[pl.ds(0, cap), :], gsel, sem.at[1]).wait()

    def gather_slice():
        for r in range(rows_per_step):
            start_row(nxt, (s - nf) * rows_per_step + r)

    @pl.when((e == 0) & (s == 0))
    def _():
        def go(r, carry):
            start_row(0, r)
            return carry
        lax.fori_loop(0, cap, go, 0, unroll=GATHER_UNROLL)

    @pl.when(s == 0)
    def _():
        wait_gather()
        u = hsel[...]
        hb[:, :half] = lax.bitcast_convert_type(u << 16, F32).astype(BF16)
        hb[:, half:] = lax.bitcast_convert_type(u & jnp.uint32(0xFFFF0000), F32).astype(BF16)
        g = gsel[...]
        mine = jnp.where(lax.broadcasted_iota(I32, g.shape, 1) == e, g, 0.0)
        gate_sc[...] = jnp.broadcast_to(jnp.sum(mine, axis=-1, keepdims=True), g.shape)

    @pl.when(s < nf)
    def _():
        x = hb[...]
        u = jnp.dot(x, wg_ref[0].astype(BF16), preferred_element_type=F32)
        v = jnp.dot(x, wu_ref[0].astype(BF16), preferred_element_type=F32)
        act[s] = (u * jax.nn.sigmoid(u) * v).astype(BF16)

    @pl.when(s >= nf)
    def _():
        gather_slice()
        o = jnp.dot(act[0], wd_ref[0, 0:tf, :].astype(BF16), preferred_element_type=F32)
        for f in range(1, nf):
            o += jnp.dot(act[f], wd_ref[0, f * tf:(f + 1) * tf, :].astype(BF16), preferred_element_type=F32)
        o_ref[...] = (o * jnp.tile(gate_sc[...], (1, o.shape[1] // LANES))).astype(o_ref.dtype)

    @pl.when((e == n_exp - 1) & (s == n_steps - 1))
    def _():
        wait_gather()


def _expert_ffn(idx_flat, hp, aff_pad, w_gate, w_up, w_down):
    n_exp, d, ff = w_gate.shape
    cap = idx_flat.shape[0] // n_exp
    half = hp.shape[1]
    tf = _pick(ff, (256, 128))
    tn = _pick(d, (512, 256, 128))
    nf, nn = ff // tf, d // tn
    assert cap % nn == 0
    last_e = n_exp - 1

    def up_map(e, s, idx):
        return (jnp.where(s < nf, e, jnp.minimum(e + 1, last_e)), 0, jnp.where(s < nf, s, 0))

    def down_map(e, s, idx):
        return (e, 0, jnp.maximum(s - nf, 0))

    def out_map(e, s, idx):
        return (e, jnp.maximum(s - nf, 0))

    grid_spec = pltpu.PrefetchScalarGridSpec(
        num_scalar_prefetch=1,
        grid=(n_exp, nf + nn),
        in_specs=[pl.BlockSpec(memory_space=pl.ANY),
                  pl.BlockSpec(memory_space=pl.ANY),
                  pl.BlockSpec((1, d, tf), up_map),
                  pl.BlockSpec((1, d, tf), up_map),
                  pl.BlockSpec((1, ff, tn), down_map)],
        out_specs=pl.BlockSpec((cap, tn), out_map),
        scratch_shapes=[pltpu.VMEM((cap, half), U32), pltpu.VMEM((cap, LANES), F32),
                        pltpu.VMEM((cap, d), BF16), pltpu.VMEM((cap, LANES), F32),
                        pltpu.VMEM((nf, cap, tf), BF16), pltpu.SemaphoreType.DMA((2,))],
    )
    return pl.pallas_call(
        functools.partial(_ffn_kernel, n_exp=n_exp, cap=cap, nf=nf, nn=nn, tf=tf, half=half),
        grid_spec=grid_spec,
        out_shape=jax.ShapeDtypeStruct((n_exp * cap, d), BF16),
        compiler_params=_params(2),
        name="expert_ffn",
    )(idx_flat, hp, aff_pad, w_gate, w_up, w_down)


def _combine_kernel(off_ref, tot_ref, o_hbm, idx_hbm, x1_ref, g2_ref, lg_ref, lb_ref, out_ref, rbuf, ibuf, acc, sem,
                    *, tb, kbuf, max_pieces, n_blocks):
    b = pl.program_id(0)
    slot = b % 2
    pieces_per_round = kbuf // PIECE

    @pl.when(b == 0)
    def _():
        rbuf[...] = jnp.zeros_like(rbuf)
        ibuf[...] = jnp.full_like(ibuf, -1)

    def copies(blk, sl, j_global, j_local):
        src = pl.multiple_of(off_ref[blk * max_pieces + j_global], PIECE)
        dst = pl.multiple_of(j_local * PIECE, PIECE)
        return (pltpu.make_async_copy(o_hbm.at[pl.ds(src, PIECE), :], rbuf.at[sl, pl.ds(dst, PIECE), :],
                                      sem.at[0, sl]),
                pltpu.make_async_copy(idx_hbm.at[pl.ds(src, PIECE), :], ibuf.at[sl, pl.ds(dst, PIECE), :],
                                      sem.at[1, sl]))

    def pieces_in_round(blk, r):
        return jnp.clip(tot_ref[blk] - r * pieces_per_round, 0, pieces_per_round)

    def start_round(blk, sl, r):
        def go(j, c):
            for cp in copies(blk, sl, r * pieces_per_round + j, j):
                cp.start()
            return c
        lax.fori_loop(0, pieces_in_round(blk, r), go, 0)

    def wait_round(blk, sl, r):
        def go(j, c):
            for cp in copies(blk, sl, r * pieces_per_round + j, j):
                cp.wait()
            return c
        lax.fori_loop(0, pieces_in_round(blk, r), go, 0)

    @pl.when(b == 0)
    def _():
        start_round(0, 0, 0)

    @pl.when(b + 1 < n_blocks)
    def _():
        start_round(b + 1, 1 - slot, 0)

    acc[...] = jnp.zeros_like(acc)
    tok = b * tb + lax.broadcasted_iota(I32, (COMBINE_GROUP, tb), 1)

    def one_round(r, carry):
        @pl.when(r > 0)
        def _():
            start_round(b, slot, r)

        wait_round(b, slot, r)
        rows = pieces_in_round(b, r) * PIECE

        def group(g, c):
            r0 = pl.multiple_of(g * COMBINE_GROUP, COMBINE_GROUP)
            k = r0 + lax.broadcasted_iota(I32, (COMBINE_GROUP, tb), 0)
            ids = jnp.tile(ibuf[slot, pl.ds(r0, COMBINE_GROUP), :], (1, tb // LANES))
            onehot = jnp.where((ids == tok) & (k < rows), 1.0, 0.0).astype(BF16)
            acc[...] += lax.dot_general(onehot, rbuf[slot, pl.ds(r0, COMBINE_GROUP), :], (((0,), (0,)), ((), ())),
                                        preferred_element_type=F32)
            return c

        lax.fori_loop(0, (rows + COMBINE_GROUP - 1) // COMBINE_GROUP, group, 0)
        return carry

    lax.fori_loop(0, (tot_ref[b] + pieces_per_round - 1) // pieces_per_round, one_round, 0)
    z = DEEPNORM_ALPHA * x1_ref[...] + g2_ref[0:1, :] * acc[...]
    out_ref[...] = _ln_rows(z, LN_EPS) * lg_ref[...] + lb_ref[...]


def _combine(expert_out, idx, x1, mods, ln_g, ln_b):
    n_exp, cap = idx.shape
    s_len, d = x1.shape
    tb = _pick(s_len, (256, 128))
    nb = s_len // tb
    max_pieces = n_exp * (cap // PIECE)
    kbuf = min(COMBINE_ROWS, max_pieces * PIECE)

    edges = jnp.arange(nb + 1, dtype=I32) * tb
    below = jnp.sum((idx[None, :, :] < edges[:, None, None]).astype(I32), axis=2)
    s0, s1 = below[:-1], below[1:]
    p0 = s0 // PIECE
    n_pc = jnp.where(s1 > s0, (s1 + PIECE - 1) // PIECE - p0, 0)
    cum = jnp.cumsum(n_pc, axis=1)
    total = cum[:, -1]
    j = jnp.arange(max_pieces, dtype=I32)[None, :]
    e_of = jnp.minimum(jnp.sum((j[:, :, None] >= cum[:, None, :]).astype(I32), axis=2), n_exp - 1)
    piece = jnp.take_along_axis(p0, e_of, axis=1) + j - jnp.take_along_axis(cum - n_pc, e_of, axis=1)
    off = jnp.where(j < total[:, None], e_of * cap + piece * PIECE, 0).astype(I32)

    idx_rep = jnp.broadcast_to(idx.reshape(n_exp * cap, 1), (n_exp * cap, LANES))
    grid_spec = pltpu.PrefetchScalarGridSpec(
        num_scalar_prefetch=2,
        grid=(nb,),
        in_specs=[pl.BlockSpec(memory_space=pl.ANY),
                  pl.BlockSpec(memory_space=pl.ANY),
                  pl.BlockSpec((tb, d), lambda i, off, tot: (i, 0)),
                  pl.BlockSpec((SUBLANES, d), lambda i, off, tot: (0, 5)),
                  pl.BlockSpec((1, d), lambda i, off, tot: (0, 0)),
                  pl.BlockSpec((1, d), lambda i, off, tot: (0, 0))],
        out_specs=pl.BlockSpec((tb, d), lambda i, off, tot: (i, 0)),
        scratch_shapes=[pltpu.VMEM((2, kbuf, d), BF16), pltpu.VMEM((2, kbuf, LANES), I32),
                        pltpu.VMEM((tb, d), F32), pltpu.SemaphoreType.DMA((2, 2))],
    )
    return pl.pallas_call(
        functools.partial(_combine_kernel, tb=tb, kbuf=kbuf, max_pieces=max_pieces, n_blocks=nb),
        grid_spec=grid_spec,
        out_shape=jax.ShapeDtypeStruct((s_len, d), F32),
        compiler_params=_params(1),
        name="combine_ln2",
    )(off.reshape(-1), total.astype(I32), expert_out, idx_rep, x1, mods, ln_g.reshape(1, d), ln_b.reshape(1, d))


def kernel(x, c, ctx, c_ctx, w_mod, b_mod, w_in, diff_lambda, diff_subln_g, w_pool, pool_scale, w_branch_attn,
           w_branch_pool, w_out, ln1_g, ln1_b, w_router, w_gate, w_up, w_down, ln2_g, ln2_b):
    batch, s_len, d = x.shape
    assert batch == 1 and w_mod.shape[0] == DEPTH
    attn_w = N_HEADS * V_DIM
    pool_w = w_pool.shape[1] * w_pool.shape[2]
    k_off, v_off, p_off = attn_w, 2 * attn_w, 3 * attn_w
    g_off = p_off + pool_w
    n_exp = w_router.shape[-1]
    cap = EC_CAPACITY_FACTOR * s_len // n_exp

    mods = _modulation(c, c_ctx, w_mod[0], b_mod[0])
    h = _premod(x[0], mods, 0)
    hc = _premod(ctx[0], mods, 1)

    w = w_in[0]
    tabs = _rope_tables(s_len)
    q = _project(h, w, 0, attn_w, BF16, tabs, HEAD_DIM ** -0.5 * math.log2(math.e))
    k = _project(h, w, k_off, attn_w, BF16, tabs)
    v = _project(h, w, v_off, attn_w, BF16)
    kc = _project(hc, w, k_off, attn_w, BF16)
    vc = _project(hc, w, v_off, attn_w, BF16)
    p = _project(h, w, p_off, pool_w, F32)
    gates = _project(h, w, g_off, 2 * d, F32)

    attn = _attention(q, k, v, kc, vc, diff_lambda[0], diff_subln_g[0])
    pooled = _pool(p, w_pool[0], pool_scale[0])
    merged = _merge(attn, pooled, w_branch_attn[0], w_branch_pool[0], gates)
    y = _project(merged, w_out[0], 0, d, F32)

    x1, hp, aff_t = _post1(x[0], y, mods, ln1_g[0], ln1_b[0], w_router[0].T)
    idx = _route(aff_t, cap).reshape(n_exp, cap)
    aff_pad = jnp.pad(aff_t.T, ((0, 0), (0, LANES - n_exp)))
    expert_out = _expert_ffn(idx.reshape(-1), hp, aff_pad, w_gate[0], w_up[0], w_down[0])
    out = _combine(expert_out, idx, x1, mods, ln2_g[0], ln2_b[0])
    return out.reshape(batch, s_len, d)
```

```python
import functools
import math

import jax
import jax.numpy as jnp
from jax import lax
from jax.experimental import pallas as pl
from jax.experimental.pallas import tpu as pltpu

F32 = jnp.float32
BF16 = jnp.bfloat16
I32 = jnp.int32
U32 = jnp.uint32

LANES = 128
SUBLANES = 8
VMEM_LIMIT = 56 * 1024 * 1024

GRID_W = 64
N_HEADS = 16
HEAD_DIM = 64
V_DIM = 2 * HEAD_DIM
POOL_WINDOWS = (2, 4, 8, 16)
POOL_HALO = 8
EC_CAPACITY_FACTOR = 2
ROPE_BASE = 10000.0
LN_EPS = 1e-5
ADALN_EPS = 1e-6
SUBLN_EPS = 1e-5
DEPTH = 1
DEEPNORM_ALPHA = (2.0 * DEPTH) ** 0.25
LAMBDA_INIT = 0.8 - 0.6 * math.exp(-0.3 * 0)
NEG_BIG = -1e30
BISECT_STEPS = 64
GATHER_UNROLL = 8
ROPE_ROW_PARTS = 8
MERGE_ROW_PARTS = 4
PIECE = 16
COMBINE_GROUP = 256
COMBINE_ROWS = 1024
LN_STRIP = 16
ATTN_STRIP = 32


def _pick(n, prefs):
    for p in prefs:
        if n % p == 0:
            return p
    raise ValueError(f"no tile for {n} in {prefs}")


def _params(n_axes):
    return pltpu.CompilerParams(dimension_semantics=("arbitrary",) * n_axes,
                                vmem_limit_bytes=VMEM_LIMIT)


def _ln_rows(x, eps):
    mu = jnp.mean(x, axis=-1, keepdims=True)
    xc = x - mu
    var = jnp.mean(xc * xc, axis=-1, keepdims=True)
    return xc * lax.rsqrt(var + eps)


def _mod_kernel(cb_ref, w_ref, b_ref, o_ref, *, tn):
    o_ref[...] = jnp.zeros_like(o_ref)
    for i in range(2):
        cv = cb_ref[i]
        s = jnp.tile(cv * jax.nn.sigmoid(cv), (1, tn // LANES))
        o_ref[i:i + 1, :] = jnp.sum(w_ref[...] * s, axis=0, keepdims=True) + b_ref[...]


def _modulation(c, c_ctx, w_mod, b_mod):
    d, n = w_mod.shape
    tn = _pick(n, (512, 256, 128))
    cb = jnp.broadcast_to(jnp.stack([c[0], c_ctx])[:, :, None], (2, d, LANES))
    return pl.pallas_call(
        functools.partial(_mod_kernel, tn=tn),
        grid=(n // tn,),
        in_specs=[pl.BlockSpec((2, d, LANES), lambda j: (0, 0, 0)),
                  pl.BlockSpec((d, tn), lambda j: (0, j)),
                  pl.BlockSpec((1, tn), lambda j: (0, j))],
        out_specs=pl.BlockSpec((SUBLANES, tn), lambda j: (0, j)),
        out_shape=jax.ShapeDtypeStruct((SUBLANES, n), F32),
        compiler_params=_params(1),
        name="modulation",
    )(cb, w_mod, b_mod.reshape(1, n))


def _premod_kernel(x_ref, sh_ref, sc_ref, o_ref, *, row):
    scale = 1.0 + sc_ref[row:row + 1, :]
    shift = sh_ref[row:row + 1, :]
    for r0 in range(0, x_ref.shape[0], LN_STRIP):
        rows = slice(r0, r0 + LN_STRIP)
        o_ref[rows, :] = (_ln_rows(x_ref[rows, :], ADALN_EPS) * scale + shift).astype(BF16)


def _premod(x2d, mods, row):
    m, d = x2d.shape
    tm = _pick(m, (256, 128))
    return pl.pallas_call(
        functools.partial(_premod_kernel, row=row),
        grid=(m // tm,),
        in_specs=[pl.BlockSpec((tm, d), lambda i: (i, 0)),
                  pl.BlockSpec((SUBLANES, d), lambda i: (0, 0)),
                  pl.BlockSpec((SUBLANES, d), lambda i: (0, 1))],
        out_specs=pl.BlockSpec((tm, d), lambda i: (i, 0)),
        out_shape=jax.ShapeDtypeStruct((m, d), BF16),
        compiler_params=_params(1),
        name="premod",
    )(x2d, mods, mods)


def _proj_kernel(*refs, rope, scale, tn):
    if rope:
        h_ref, w_ref, cos_ref, sa_ref, sb_ref, o_ref, wbf = refs
    else:
        h_ref, w_ref, o_ref, wbf = refs

    @pl.when(pl.program_id(1) == 0)
    def _():
        wbf[...] = w_ref[...].astype(BF16)

    if rope:
        half = h_ref.shape[0] // ROPE_ROW_PARTS
        for r0 in range(0, h_ref.shape[0], half):
            rows = slice(r0, r0 + half)
            acc = jnp.dot(h_ref[rows, :], wbf[...], preferred_element_type=F32)
            cos, sa, sb = cos_ref[rows, :], sa_ref[rows, :], sb_ref[rows, :]
            for c in range(tn // LANES):
                cols = slice(c * LANES, (c + 1) * LANES)
                a = acc[:, cols]
                r = a * cos + pltpu.roll(a, LANES - 16, 1) * sa + pltpu.roll(a, 16, 1) * sb
                o_ref[rows, cols] = (r * scale).astype(o_ref.dtype)
    else:
        o_ref[...] = jnp.dot(h_ref[...], wbf[...], preferred_element_type=F32).astype(o_ref.dtype)


def _project(h, w, col_off, n_cols, out_dtype, rope_tabs=None, scale=1.0):
    m, k = h.shape
    tn = _pick(math.gcd(n_cols, col_off), (512, 256, 128))
    tm = _pick(m, (1024, 512, 256, 128))
    off = col_off // tn
    in_specs = [pl.BlockSpec((tm, k), lambda j, i: (i, 0)),
                pl.BlockSpec((k, tn), lambda j, i: (0, j + off))]
    args = [h, w]
    if rope_tabs is not None:
        in_specs += [pl.BlockSpec((tm, LANES), lambda j, i: (i, 0))] * 3
        args += list(rope_tabs)
    return pl.pallas_call(
        functools.partial(_proj_kernel, rope=rope_tabs is not None, scale=scale, tn=tn),
        grid=(n_cols // tn, m // tm),
        in_specs=in_specs,
        out_specs=pl.BlockSpec((tm, tn), lambda j, i: (i, j)),
        out_shape=jax.ShapeDtypeStruct((m, n_cols), out_dtype),
        scratch_shapes=[pltpu.VMEM((k, tn), BF16)],
        compiler_params=_params(2),
        name="project",
    )(*args)


def _rope_tables(n_tokens):
    rows = n_tokens // GRID_W
    row = jnp.broadcast_to(jnp.arange(rows, dtype=F32)[:, None], (rows, GRID_W)).reshape(-1)
    col = jnp.broadcast_to(jnp.arange(GRID_W, dtype=F32)[None, :], (rows, GRID_W)).reshape(-1)
    axis_dim = HEAD_DIM // 2
    nf = axis_dim // 2
    inv_freq = ROPE_BASE ** (-jnp.arange(0, axis_dim, 2, dtype=F32) / axis_dim)
    ang = jnp.stack([row[:, None] * inv_freq, col[:, None] * inv_freq], axis=1)
    cos, sin = jnp.cos(ang), jnp.sin(ang)
    zero = jnp.zeros_like(sin)
    cos64 = jnp.concatenate([cos, cos], axis=-1).reshape(n_tokens, HEAD_DIM)
    sa64 = jnp.concatenate([-sin, zero], axis=-1).reshape(n_tokens, HEAD_DIM)
    sb64 = jnp.concatenate([zero, sin], axis=-1).reshape(n_tokens, HEAD_DIM)
    rep = LANES // HEAD_DIM
    return tuple(jnp.tile(t, (1, rep)) for t in (cos64, sa64, sb64))


def _attn_kernel(q_ref, k_ref, v_ref, kc_ref, vc_ref, dl_ref, g_ref, o_ref, qs_sc, vx_sc, sa_sc, sb_sc, sc_sc,
                 pa_sc, pb_sc, xa_sc, xb_sc, xc_sc, m_sc, acc_sc, *, tq, tk, n_chunks):
    s_len = n_chunks * tk
    c_len = kc_ref.shape[0]

    @pl.when(pl.program_id(1) == 0)
    def _():
        vx_sc[0:s_len, 0:V_DIM] = v_ref[...]
        vx_sc[s_len:s_len + c_len, 0:V_DIM] = vc_ref[...]
        vx_sc[:, V_DIM:2 * V_DIM] = jnp.ones((s_len + c_len, V_DIM), BF16)

    q = q_ref[...]
    lane = lax.broadcasted_iota(I32, q.shape, 1)
    zero = jnp.zeros_like(q)
    qs_sc[0:tq, :] = jnp.where(lane < HEAD_DIM, q, zero)
    qs_sc[tq:2 * tq, :] = jnp.where(lane >= HEAD_DIM, q, zero)
    m_sc[...] = jnp.full_like(m_sc, NEG_BIG)
    acc_sc[...] = jnp.zeros_like(acc_sc)

    def scores(s_ref, x_ref, kb):
        s = lax.dot_general(qs_sc[...], kb, (((1,), (1,)), ((), ())), preferred_element_type=F32)
        s_ref[...] = s
        x_ref[...] = jnp.broadcast_to(jnp.max(s, axis=-1, keepdims=True), x_ref.shape)

    def absorb(s_ref, x_ref, p_ref, v0):
        width = s_ref.shape[1]
        for r0 in range(0, 2 * tq, ATTN_STRIP):
            rows = slice(r0, r0 + ATTN_STRIP)
            m_prev = m_sc[rows, :]
            m_new = jnp.maximum(m_prev, x_ref[rows, :])
            x_ref[rows, :] = jnp.exp2(m_prev - m_new)
            m_sc[rows, :] = m_new
            p_ref[rows, 0:width] = jnp.exp2((s_ref[rows, :] - jnp.tile(m_new, (1, width // LANES))).astype(BF16))
        pv = jnp.dot(p_ref[:, 0:width], vx_sc[pl.ds(v0, width), :], preferred_element_type=F32)
        acc_sc[...] = jnp.tile(x_ref[...], (1, 2)) * acc_sc[...] + pv

    def kchunk(c):
        return k_ref[pl.ds(pl.multiple_of(c * tk, tk), tk), :]

    scores(sa_sc, xa_sc, kchunk(0))

    def body(j, carry):
        c = 2 * j
        scores(sb_sc, xb_sc, kchunk(c + 1))
        absorb(sa_sc, xa_sc, pa_sc, pl.multiple_of(c * tk, tk))
        scores(sa_sc, xa_sc, kchunk(c + 2))
        absorb(sb_sc, xb_sc, pb_sc, pl.multiple_of((c + 1) * tk, tk))
        return carry

    lax.fori_loop(0, (n_chunks - 2) // 2, body, 0)
    scores(sb_sc, xb_sc, kchunk(n_chunks - 1))
    absorb(sa_sc, xa_sc, pa_sc, (n_chunks - 2) * tk)
    scores(sc_sc, xc_sc, kc_ref[...])
    absorb(sb_sc, xb_sc, pb_sc, (n_chunks - 1) * tk)
    absorb(sc_sc, xc_sc, pa_sc, s_len)

    dl = dl_ref[...]
    lam = (jnp.exp(jnp.sum(dl[0:1] * dl[1:2], axis=-1, keepdims=True))
           - jnp.exp(jnp.sum(dl[2:3] * dl[3:4], axis=-1, keepdims=True)) + LAMBDA_INIT)
    o = (acc_sc[0:tq, 0:V_DIM] / acc_sc[0:tq, V_DIM:2 * V_DIM]
         - lam * (acc_sc[tq:2 * tq, 0:V_DIM] / acc_sc[tq:2 * tq, V_DIM:2 * V_DIM]))
    o = o * lax.rsqrt(jnp.mean(o * o, axis=-1, keepdims=True) + SUBLN_EPS)
    o_ref[...] = (o * g_ref[...] * (1.0 - LAMBDA_INIT)).astype(o_ref.dtype)


def _attention(q, k, v, kc, vc, diff_lambda, subln_g):
    s_len, width = q.shape
    c_len = kc.shape[0]
    tq = _pick(s_len, (512, 256, 128))
    tk = _pick(s_len // 2, (2048, 1024, 512, 256, 128))
    n_chunks = s_len // tk
    assert n_chunks % 2 == 0
    n_heads = width // V_DIM
    return pl.pallas_call(
        functools.partial(_attn_kernel, tq=tq, tk=tk, n_chunks=n_chunks),
        grid=(n_heads, s_len // tq),
        in_specs=[pl.BlockSpec((tq, V_DIM), lambda h, i: (i, h)),
                  pl.BlockSpec((s_len, V_DIM), lambda h, i: (0, h)),
                  pl.BlockSpec((s_len, V_DIM), lambda h, i: (0, h)),
                  pl.BlockSpec((c_len, V_DIM), lambda h, i: (0, h)),
                  pl.BlockSpec((c_len, V_DIM), lambda h, i: (0, h)),
                  pl.BlockSpec((4, HEAD_DIM), lambda h, i: (0, 0)),
                  pl.BlockSpec((1, V_DIM), lambda h, i: (0, 0))],
        out_specs=pl.BlockSpec((tq, V_DIM), lambda h, i: (i, h)),
        out_shape=jax.ShapeDtypeStruct((s_len, width), BF16),
        scratch_shapes=[pltpu.VMEM((2 * tq, V_DIM), BF16), pltpu.VMEM((s_len + c_len, 2 * V_DIM), BF16),
                        pltpu.VMEM((2 * tq, tk), F32), pltpu.VMEM((2 * tq, tk), F32),
                        pltpu.VMEM((2 * tq, c_len), F32)]
                       + [pltpu.VMEM((2 * tq, tk), BF16)] * 2
                       + [pltpu.VMEM((2 * tq, LANES), F32)] * 4
                       + [pltpu.VMEM((2 * tq, 2 * V_DIM), F32)],
        compiler_params=_params(2),
        name="diff_attention",
    )(q, k, v, kc, vc, diff_lambda, subln_g.reshape(1, V_DIM))


def _pool_kernel(p_ref, prev_ref, next_ref, w_ref, sc_ref, o_ref, ext, *, tm, n_tokens, gdim):
    i = pl.program_id(0)
    last = pl.num_programs(0) - 1
    ext[0:POOL_HALO, :] = jnp.where(i > 0, prev_ref[...], 0.0)
    ext[POOL_HALO:POOL_HALO + tm, :] = p_ref[...]
    ext[POOL_HALO + tm:2 * POOL_HALO + tm, :] = jnp.where(i < last, next_ref[...], 0.0)
    t = i * tm + lax.broadcasted_iota(I32, (tm, 1), 0)
    for g, w in enumerate(POOL_WINDOWS):
        cols = slice(g * gdim, (g + 1) * gdim)
        ssum = None
        for d in range(-(w // 2), w // 2):
            part = ext[POOL_HALO + d:POOL_HALO + d + tm, cols]
            ssum = part if ssum is None else ssum + part
        cnt = jnp.minimum(t + w // 2, n_tokens) - jnp.maximum(t - w // 2, 0)
        mean = ssum / cnt.astype(F32)
        mg = (mean - p_ref[:, cols]).astype(BF16)
        y = jnp.dot(mg, w_ref[g].astype(BF16), preferred_element_type=F32)
        o_ref[:, cols] = (y * sc_ref[:, cols]).astype(o_ref.dtype)


def _pool(p, w_pool, pool_scale):
    s_len, width = p.shape
    n_groups, gdim, _ = w_pool.shape
    assert n_groups == len(POOL_WINDOWS) and max(POOL_WINDOWS) // 2 <= POOL_HALO
    tm = _pick(s_len, (512, 256, 128))
    hb = tm // POOL_HALO
    n_halo_blocks = s_len // POOL_HALO
    return pl.pallas_call(
        functools.partial(_pool_kernel, tm=tm, n_tokens=s_len, gdim=gdim),
        grid=(s_len // tm,),
        in_specs=[pl.BlockSpec((tm, width), lambda i: (i, 0)),
                  pl.BlockSpec((POOL_HALO, width), lambda i: (jnp.maximum(i * hb - 1, 0), 0)),
                  pl.BlockSpec((POOL_HALO, width), lambda i: (jnp.minimum((i + 1) * hb, n_halo_blocks - 1), 0)),
                  pl.BlockSpec((n_groups, gdim, gdim), lambda i: (0, 0, 0)),
                  pl.BlockSpec((1, width), lambda i: (0, 0))],
        out_specs=pl.BlockSpec((tm, width), lambda i: (i, 0)),
        out_shape=jax.ShapeDtypeStruct((s_len, width), BF16),
        scratch_shapes=[pltpu.VMEM((tm + 2 * POOL_HALO, width), F32)],
        compiler_params=_params(1),
        name="pool_mixer",
    )(p, p, p, w_pool, pool_scale.reshape(1, width))


def _merge_kernel(a_ref, p_ref, wa_ref, wb_ref, ga_ref, gb_ref, o_ref, wabf, wbbf):
    @pl.when(pl.program_id(1) == 0)
    def _():
        wabf[...] = wa_ref[...].astype(BF16)
        wbbf[...] = wb_ref[...].astype(BF16)

    part = a_ref.shape[0] // MERGE_ROW_PARTS
    for r0 in range(0, a_ref.shape[0], part):
        rows = slice(r0, r0 + part)
        ya = jnp.dot(a_ref[rows, :], wabf[...], preferred_element_type=F32)
        yb = jnp.dot(p_ref[rows, :], wbbf[...], preferred_element_type=F32)
        o_ref[rows, :] = (jax.nn.sigmoid(ga_ref[rows, :]) * ya
                          + jax.nn.sigmoid(gb_ref[rows, :]) * yb).astype(o_ref.dtype)


def _merge(attn, pooled, w_a, w_b, gates):
    m, ka = attn.shape
    kb = pooled.shape[1]
    n = w_a.shape[1]
    tn = _pick(n, (512, 256, 128))
    tm = _pick(m, (1024, 512, 256, 128))
    gb_off = n // tn
    return pl.pallas_call(
        _merge_kernel,
        grid=(n // tn, m // tm),
        in_specs=[pl.BlockSpec((tm, ka), lambda j, i: (i, 0)),
                  pl.BlockSpec((tm, kb), lambda j, i: (i, 0)),
                  pl.BlockSpec((ka, tn), lambda j, i: (0, j)),
                  pl.BlockSpec((kb, tn), lambda j, i: (0, j)),
                  pl.BlockSpec((tm, tn), lambda j, i: (i, j)),
                  pl.BlockSpec((tm, tn), lambda j, i: (i, j + gb_off))],
        out_specs=pl.BlockSpec((tm, tn), lambda j, i: (i, j)),
        out_shape=jax.ShapeDtypeStruct((m, n), BF16),
        scratch_shapes=[pltpu.VMEM((ka, tn), BF16), pltpu.VMEM((kb, tn), BF16)],
        compiler_params=_params(2),
        name="gated_merge",
    )(attn, pooled, w_a, w_b, gates, gates)


def _post1_kernel(x_ref, y_ref, g1_ref, sh_ref, sc_ref, lg_ref, lb_ref, wr_ref, x1_ref, hp_ref, aff_ref, *, half):
    z = DEEPNORM_ALPHA * x_ref[...] + g1_ref[0:1, :] * y_ref[...]
    x1 = _ln_rows(z, LN_EPS) * lg_ref[...] + lb_ref[...]
    x1_ref[...] = x1
    h2 = _ln_rows(x1, ADALN_EPS) * (1.0 + sc_ref[0:1, :]) + sh_ref[0:1, :]
    hb = lax.bitcast_convert_type(h2.astype(BF16).astype(F32), U32)
    hp_ref[...] = (hb[:, :half] >> 16) | (hb[:, half:] & jnp.uint32(0xFFFF0000))
    logits = lax.dot_general(wr_ref[...], h2, (((1,), (1,)), ((), ())),
                             precision=lax.Precision.HIGHEST, preferred_element_type=F32)
    mx = jnp.max(logits, axis=0, keepdims=True)
    ex = jnp.exp(logits - mx)
    aff_ref[...] = ex / jnp.sum(ex, axis=0, keepdims=True)


def _post1(x2d, y, mods, ln_g, ln_b, w_router_t):
    m, d = x2d.shape
    n_exp = w_router_t.shape[0]
    tm = _pick(m, (256, 128))
    row = lambda i: (i, 0)
    vec = lambda i: (0, 0)
    return pl.pallas_call(
        functools.partial(_post1_kernel, half=d // 2),
        grid=(m // tm,),
        in_specs=[pl.BlockSpec((tm, d), row), pl.BlockSpec((tm, d), row),
                  pl.BlockSpec((SUBLANES, d), lambda i: (0, 2)),
                  pl.BlockSpec((SUBLANES, d), lambda i: (0, 3)),
                  pl.BlockSpec((SUBLANES, d), lambda i: (0, 4)),
                  pl.BlockSpec((1, d), vec), pl.BlockSpec((1, d), vec),
                  pl.BlockSpec((n_exp, d), vec)],
        out_specs=[pl.BlockSpec((tm, d), row), pl.BlockSpec((tm, d // 2), row),
                   pl.BlockSpec((n_exp, tm), lambda i: (0, i))],
        out_shape=[jax.ShapeDtypeStruct((m, d), F32), jax.ShapeDtypeStruct((m, d // 2), U32),
                   jax.ShapeDtypeStruct((n_exp, m), F32)],
        compiler_params=_params(1),
        name="ln1_router",
    )(x2d, y, mods, mods, mods, ln_g.reshape(1, d), ln_b.reshape(1, d), w_router_t)


def _route_kernel(aff_ref, idx_ref, csel_sc, *, n_exp, nc, cap, sblk):
    rows = n_exp * nc
    aff3 = aff_ref[...]
    ri = lax.broadcasted_iota(I32, (rows, rows), 0)
    rj = lax.broadcasted_iota(I32, (rows, rows), 1)
    before = jnp.where(((ri // nc) == (rj // nc)) & (rj < ri), 1.0, 0.0).astype(BF16)
    li = lax.broadcasted_iota(I32, (LANES, LANES), 0)
    lj = lax.broadcasted_iota(I32, (LANES, LANES), 1)
    tri = jnp.where(li <= lj, 1.0, 0.0).astype(BF16)
    ones = jnp.ones((LANES, LANES), BF16)

    def per_expert_sum(x3):
        return jnp.sum(jnp.sum(x3, axis=1, keepdims=True), axis=2, keepdims=True)

    def count(mask3):
        return per_expert_sum(jnp.where(mask3, 1.0, 0.0))

    def prefix(mask2):
        mb = jnp.where(mask2, 1.0, 0.0).astype(BF16)
        within = jnp.dot(mb, tri, preferred_element_type=F32)
        row_tot = jnp.dot(mb, ones, preferred_element_type=F32).astype(BF16)
        return within + jnp.dot(before, row_tot, preferred_element_type=F32)

    def rows_of(x3):
        return jnp.broadcast_to(x3, (n_exp, nc, LANES)).reshape(rows, LANES)

    top = jnp.max(jnp.max(aff3, axis=1, keepdims=True), axis=2, keepdims=True)
    lo0 = jnp.zeros_like(top)
    hi0 = 2.0 * top + 1e-30

    def bisect(_, carry):
        lo, hi = carry
        mid = 0.5 * (lo + hi)
        ok = count(aff3 >= mid) >= cap
        return jnp.where(ok, mid, lo), jnp.where(ok, hi, mid)

    thr3, _ = lax.fori_loop(0, BISECT_STEPS, bisect, (lo0, hi0))
    need = rows_of(cap - count(aff3 > thr3))
    aff2 = aff3.reshape(rows, LANES)
    thr2 = rows_of(thr3)
    gt = aff2 > thr2
    eq = aff2 == thr2
    sel = gt | (eq & (prefix(eq) <= need))
    csel_sc[...] = jnp.where(sel, prefix(sel), 0.0)

    lane_i = lax.broadcasted_iota(I32, (1, LANES), 1)
    slot0 = lax.broadcasted_iota(I32, (sblk, LANES), 0).astype(F32) + 1.0

    def per_expert(e, carry):
        for sb in range(cap // sblk):
            slot = slot0 + float(sb * sblk)

            def per_chunk(c, acc):
                hit = csel_sc[pl.ds(e * nc + c, 1), :] == slot
                return acc + jnp.where(hit, (lane_i + c * LANES).astype(F32), 0.0)

            acc = lax.fori_loop(0, nc, per_chunk, jnp.zeros((sblk, LANES), F32))
            idx_ref[e, sb * sblk:(sb + 1) * sblk, :] = jnp.sum(acc, axis=-1, keepdims=True).astype(I32)
        return carry

    lax.fori_loop(0, n_exp, per_expert, 0)


def _route(aff_t, cap):
    n_exp, s_len = aff_t.shape
    nc = s_len // LANES
    rows = n_exp * nc
    sblk = _pick(cap, (256, 128))
    return pl.pallas_call(
        functools.partial(_route_kernel, n_exp=n_exp, nc=nc, cap=cap, sblk=sblk),
        grid=(1,),
        in_specs=[pl.BlockSpec((n_exp, nc, LANES), lambda i: (0, 0, 0))],
        out_specs=pl.BlockSpec((n_exp, cap, 1), lambda i: (0, 0, 0)),
        out_shape=jax.ShapeDtypeStruct((n_exp, cap, 1), I32),
        scratch_shapes=[pltpu.VMEM((rows, LANES), F32)],
        compiler_params=_params(1),
        name="expert_choice",
    )(aff_t.reshape(n_exp, nc, LANES))


def _ffn_kernel(idx_ref, hp_hbm, aff_hbm, wg_ref, wu_ref, wd_ref, o_ref, hsel, gsel, hb, gate_sc, act, sem,
                *, n_exp, cap, nf, nn, tf, half):
    e = pl.program_id(0)
    s = pl.program_id(1)
    n_steps = nf + nn
    rows_per_step = cap // nn
    nxt = jnp.minimum(e + 1, n_exp - 1)

    def start_row(ee, r):
        t = idx_ref[ee * cap + r]
        pltpu.make_async_copy(hp_hbm.at[pl.ds(t, 1), :], hsel.at[pl.ds(r, 1), :], sem.at[0]).start()
        pltpu.make_async_copy(aff_hbm.at[pl.ds(t, 1), :], gsel.at[pl.ds(r, 1), :], sem.at[1]).start()

    def wait_gather():
        pltpu.make_async_copy(hp_hbm.at[pl.ds(0, cap), :], hsel, sem.at[0]).wait()
        pltpu.make_async_copy(aff_hbm.at[pl.ds(0, cap), :], gsel, sem.at[1]).wait()

    def gather_slice():
        for r in range(rows_per_step):
            start_row(nxt, (s - nf) * rows_per_step + r)

    @pl.when((e == 0) & (s == 0))
    def _():
        def go(r, carry):
            start_row(0, r)
            return carry
        lax.fori_loop(0, cap, go, 0, unroll=GATHER_UNROLL)

    @pl.when(s == 0)
    def _():
        wait_gather()
        u = hsel[...]
        hb[:, :half] = lax.bitcast_convert_type(u << 16, F32).astype(BF16)
        hb[:, half:] = lax.bitcast_convert_type(u & jnp.uint32(0xFFFF0000), F32).astype(BF16)
        g = gsel[...]
        mine = jnp.where(lax.broadcasted_iota(I32, g.shape, 1) == e, g, 0.0)
        gate_sc[...] = jnp.broadcast_to(jnp.sum(mine, axis=-1, keepdims=True), g.shape)

    @pl.when(s < nf)
    def _():
        x = hb[...]
        u = jnp.dot(x, wg_ref[0].astype(BF16), preferred_element_type=F32)
        v = jnp.dot(x, wu_ref[0].astype(BF16), preferred_element_type=F32)
        act[s] = (u * jax.nn.sigmoid(u) * v).astype(BF16)

    @pl.when(s >= nf)
    def _():
        gather_slice()
        o = jnp.dot(act[0], wd_ref[0, 0:tf, :].astype(BF16), preferred_element_type=F32)
        for f in range(1, nf):
            o += jnp.dot(act[f], wd_ref[0, f * tf:(f + 1) * tf, :].astype(BF16), preferred_element_type=F32)
        o_ref[...] = (o * jnp.tile(gate_sc[...], (1, o.shape[1] // LANES))).astype(o_ref.dtype)

    @pl.when((e == n_exp - 1) & (s == n_steps - 1))
    def _():
        wait_gather()


def _expert_ffn(idx_flat, hp, aff_pad, w_gate, w_up, w_down):
    n_exp, d, ff = w_gate.shape
    cap = idx_flat.shape[0] // n_exp
    half = hp.shape[1]
    tf = _pick(ff, (256, 128))
    tn = _pick(d, (512, 256, 128))
    nf, nn = ff // tf, d // tn
    assert cap % nn == 0
    last_e = n_exp - 1

    def up_map(e, s, idx):
        return (jnp.where(s < nf, e, jnp.minimum(e + 1, last_e)), 0, jnp.where(s < nf, s, 0))

    def down_map(e, s, idx):
        return (e, 0, jnp.maximum(s - nf, 0))

    def out_map(e, s, idx):
        return (e, jnp.maximum(s - nf, 0))

    grid_spec = pltpu.PrefetchScalarGridSpec(
        num_scalar_prefetch=1,
        grid=(n_exp, nf + nn),
        in_specs=[pl.BlockSpec(memory_space=pl.ANY),
                  pl.BlockSpec(memory_space=pl.ANY),
                  pl.BlockSpec((1, d, tf), up_map),
                  pl.BlockSpec((1, d, tf), up_map),
                  pl.BlockSpec((1, ff, tn), down_map)],
        out_specs=pl.BlockSpec((cap, tn), out_map),
        scratch_shapes=[pltpu.VMEM((cap, half), U32), pltpu.VMEM((cap, LANES), F32),
                        pltpu.VMEM((cap, d), BF16), pltpu.VMEM((cap, LANES), F32),
                        pltpu.VMEM((nf, cap, tf), BF16), pltpu.SemaphoreType.DMA((2,))],
    )
    return pl.pallas_call(
        functools.partial(_ffn_kernel, n_exp=n_exp, cap=cap, nf=nf, nn=nn, tf=tf, half=half),
        grid_spec=grid_spec,
        out_shape=jax.ShapeDtypeStruct((n_exp * cap, d), BF16),
        compiler_params=_params(2),
        name="expert_ffn",
    )(idx_flat, hp, aff_pad, w_gate, w_up, w_down)


def _combine_kernel(off_ref, tot_ref, o_hbm, idx_hbm, x1_ref, g2_ref, lg_ref, lb_ref, out_ref, rbuf, ibuf, acc, sem,
                    *, tb, kbuf, max_pieces, n_blocks):
    b = pl.program_id(0)
    slot = b % 2
    pieces_per_round = kbuf // PIECE

    @pl.when(b == 0)
    def _():
        rbuf[...] = jnp.zeros_like(rbuf)
        ibuf[...] = jnp.full_like(ibuf, -1)

    def copies(blk, sl, j_global, j_local):
        src = pl.multiple_of(off_ref[blk * max_pieces + j_global], PIECE)
        dst = pl.multiple_of(j_local * PIECE, PIECE)
        return (pltpu.make_async_copy(o_hbm.at[pl.ds(src, PIECE), :], rbuf.at[sl, pl.ds(dst, PIECE), :],
                                      sem.at[0, sl]),
                pltpu.make_async_copy(idx_hbm.at[pl.ds(src, PIECE), :], ibuf.at[sl, pl.ds(dst, PIECE), :],
                                      sem.at[1, sl]))

    def pieces_in_round(blk, r):
        return jnp.clip(tot_ref[blk] - r * pieces_per_round, 0, pieces_per_round)

    def start_round(blk, sl, r):
        def go(j, c):
            for cp in copies(blk, sl, r * pieces_per_round + j, j):
                cp.start()
            return c
        lax.fori_loop(0, pieces_in_round(blk, r), go, 0)

    def wait_round(blk, sl, r):
        def go(j, c):
            for cp in copies(blk, sl, r * pieces_per_round + j, j):
                cp.wait()
            return c
        lax.fori_loop(0, pieces_in_round(blk, r), go, 0)

    @pl.when(b == 0)
    def _():
        start_round(0, 0, 0)

    @pl.when(b + 1 < n_blocks)
    def _():
        start_round(b + 1, 1 - slot, 0)

    acc[...] = jnp.zeros_like(acc)
    tok = b * tb + lax.broadcasted_iota(I32, (COMBINE_GROUP, tb), 1)

    def one_round(r, carry):
        @pl.when(r > 0)
        def _():
            start_round(b, slot, r)

        wait_round(b, slot, r)
        rows = pieces_in_round(b, r) * PIECE

        def group(g, c):
            r0 = pl.multiple_of(g * COMBINE_GROUP, COMBINE_GROUP)
            k = r0 + lax.broadcasted_iota(I32, (COMBINE_GROUP, tb), 0)
            ids = jnp.tile(ibuf[slot, pl.ds(r0, COMBINE_GROUP), :], (1, tb // LANES))
            onehot = jnp.where((ids == tok) & (k < rows), 1.0, 0.0).astype(BF16)
            acc[...] += lax.dot_general(onehot, rbuf[slot, pl.ds(r0, COMBINE_GROUP), :], (((0,), (0,)), ((), ())),
                                        preferred_element_type=F32)
            return c

        lax.fori_loop(0, (rows + COMBINE_GROUP - 1) // COMBINE_GROUP, group, 0)
        return carry

    lax.fori_loop(0, (tot_ref[b] + pieces_per_round - 1) // pieces_per_round, one_round, 0)
    z = DEEPNORM_ALPHA * x1_ref[...] + g2_ref[0:1, :] * acc[...]
    out_ref[...] = _ln_rows(z, LN_EPS) * lg_ref[...] + lb_ref[...]


def _combine(expert_out, idx, x1, mods, ln_g, ln_b):
    n_exp, cap = idx.shape
    s_len, d = x1.shape
    tb = _pick(s_len, (256, 128))
    nb = s_len // tb
    max_pieces = n_exp * (cap // PIECE)
    kbuf = min(COMBINE_ROWS, max_pieces * PIECE)

    edges = jnp.arange(nb + 1, dtype=I32) * tb
    below = jnp.sum((idx[None, :, :] < edges[:, None, None]).astype(I32), axis=2)
    s0, s1 = below[:-1], below[1:]
    p0 = s0 // PIECE
    n_pc = jnp.where(s1 > s0, (s1 + PIECE - 1) // PIECE - p0, 0)
    cum = jnp.cumsum(n_pc, axis=1)
    total = cum[:, -1]
    j = jnp.arange(max_pieces, dtype=I32)[None, :]
    e_of = jnp.minimum(jnp.sum((j[:, :, None] >= cum[:, None, :]).astype(I32), axis=2), n_exp - 1)
    mine = e_of[:, :, None] == jnp.arange(n_exp, dtype=I32)[None, None, :]
    first_piece = jnp.sum(jnp.where(mine, (p0 - (cum - n_pc))[:, None, :], 0), axis=2)
    off = jnp.where(j < total[:, None], e_of * cap + (first_piece + j) * PIECE, 0).astype(I32)

    idx_rep = jnp.broadcast_to(idx.reshape(n_exp * cap, 1), (n_exp * cap, LANES))
    grid_spec = pltpu.PrefetchScalarGridSpec(
        num_scalar_prefetch=2,
        grid=(nb,),
        in_specs=[pl.BlockSpec(memory_space=pl.ANY),
                  pl.BlockSpec(memory_space=pl.ANY),
                  pl.BlockSpec((tb, d), lambda i, off, tot: (i, 0)),
                  pl.BlockSpec((SUBLANES, d), lambda i, off, tot: (0, 5)),
                  pl.BlockSpec((1, d), lambda i, off, tot: (0, 0)),
                  pl.BlockSpec((1, d), lambda i, off, tot: (0, 0))],
        out_specs=pl.BlockSpec((tb, d), lambda i, off, tot: (i, 0)),
        scratch_shapes=[pltpu.VMEM((2, kbuf, d), BF16), pltpu.VMEM((2, kbuf, LANES), I32),
                        pltpu.VMEM((tb, d), F32), pltpu.SemaphoreType.DMA((2, 2))],
    )
    return pl.pallas_call(
        functools.partial(_combine_kernel, tb=tb, kbuf=kbuf, max_pieces=max_pieces, n_blocks=nb),
        grid_spec=grid_spec,
        out_shape=jax.ShapeDtypeStruct((s_len, d), F32),
        compiler_params=_params(1),
        name="combine_ln2",
    )(off.reshape(-1), total.astype(I32), expert_out, idx_rep, x1, mods, ln_g.reshape(1, d), ln_b.reshape(1, d))


def kernel(x, c, ctx, c_ctx, w_mod, b_mod, w_in, diff_lambda, diff_subln_g, w_pool, pool_scale, w_branch_attn,
           w_branch_pool, w_out, ln1_g, ln1_b, w_router, w_gate, w_up, w_down, ln2_g, ln2_b):
    batch, s_len, d = x.shape
    assert batch == 1 and w_mod.shape[0] == DEPTH
    attn_w = N_HEADS * V_DIM
    pool_w = w_pool.shape[1] * w_pool.shape[2]
    k_off, v_off, p_off = attn_w, 2 * attn_w, 3 * attn_w
    g_off = p_off + pool_w
    n_exp = w_router.shape[-1]
    cap = EC_CAPACITY_FACTOR * s_len // n_exp

    mods = _modulation(c, c_ctx, w_mod[0], b_mod[0])
    h = _premod(x[0], mods, 0)
    hc = _premod(ctx[0], mods, 1)

    w = w_in[0]
    tabs = _rope_tables(s_len)
    q = _project(h, w, 0, attn_w, BF16, tabs, HEAD_DIM ** -0.5 * math.log2(math.e))
    k = _project(h, w, k_off, attn_w, BF16, tabs)
    v = _project(h, w, v_off, attn_w, BF16)
    kc = _project(hc, w, k_off, attn_w, BF16)
    vc = _project(hc, w, v_off, attn_w, BF16)
    p = _project(h, w, p_off, pool_w, F32)
    gates = _project(h, w, g_off, 2 * d, F32)

    attn = _attention(q, k, v, kc, vc, diff_lambda[0], diff_subln_g[0])
    pooled = _pool(p, w_pool[0], pool_scale[0])
    merged = _merge(attn, pooled, w_branch_attn[0], w_branch_pool[0], gates)
    y = _project(merged, w_out[0], 0, d, F32)

    x1, hp, aff_t = _post1(x[0], y, mods, ln1_g[0], ln1_b[0], w_router[0].T)
    idx = _route(aff_t, cap).reshape(n_exp, cap)
    aff_pad = jnp.pad(aff_t.T, ((0, 0), (0, LANES - n_exp)))
    expert_out = _expert_ffn(idx.reshape(-1), hp, aff_pad, w_gate[0], w_up[0], w_down[0])
    out = _combine(expert_out, idx, x1, mods, ln2_g[0], ln2_b[0])
    return out.reshape(batch, s_len, d)
```

```python
import functools
import math

import jax
import jax.numpy as jnp
from jax import lax
from jax.experimental import pallas as pl
from jax.experimental.pallas import tpu as pltpu

F32 = jnp.float32
BF16 = jnp.bfloat16
I32 = jnp.int32
U32 = jnp.uint32

LANES = 128
SUBLANES = 8
VMEM_LIMIT = 56 * 1024 * 1024

GRID_W = 64
N_HEADS = 16
HEAD_DIM = 64
V_DIM = 2 * HEAD_DIM
POOL_WINDOWS = (2, 4, 8, 16)
POOL_HALO = 8
EC_CAPACITY_FACTOR = 2
ROPE_BASE = 10000.0
LN_EPS = 1e-5
ADALN_EPS = 1e-6
SUBLN_EPS = 1e-5
DEPTH = 1
DEEPNORM_ALPHA = (2.0 * DEPTH) ** 0.25
LAMBDA_INIT = 0.8 - 0.6 * math.exp(-0.3 * 0)
NEG_BIG = -1e30
BISECT_STEPS = 64
GATHER_UNROLL = 8
ROPE_ROW_PARTS = 8
MERGE_ROW_PARTS = 4
PIECE = 16
COMBINE_GROUP = 256
COMBINE_ROWS = 1024
LN_STRIP = 16
ATTN_STRIP = 32


def _pick(n, prefs):
    for p in prefs:
        if n % p == 0:
            return p
    raise ValueError(f"no tile for {n} in {prefs}")


def _params(n_axes):
    return pltpu.CompilerParams(dimension_semantics=("arbitrary",) * n_axes,
                                vmem_limit_bytes=VMEM_LIMIT)


def _ln_rows(x, eps):
    mu = jnp.mean(x, axis=-1, keepdims=True)
    xc = x - mu
    var = jnp.mean(xc * xc, axis=-1, keepdims=True)
    return xc * lax.rsqrt(var + eps)


def _mod_kernel(cb_ref, w_ref, b_ref, o_ref, *, tn):
    @pl.when(pl.program_id(1) == 0)
    def _():
        o_ref[...] = jnp.broadcast_to(b_ref[...], o_ref.shape)

    w = w_ref[...]
    for i in range(2):
        cv = cb_ref[i]
        s = jnp.tile(cv * jax.nn.sigmoid(cv), (1, tn // LANES))
        o_ref[i:i + 1, :] += jnp.sum(w * s, axis=0, keepdims=True)


def _modulation(c, c_ctx, w_mod, b_mod):
    d, n = w_mod.shape
    tn = _pick(n, (4096, 2048, 1024, 512, 256, 128))
    tk = _pick(d, (512, 256, 128))
    cb = jnp.broadcast_to(jnp.stack([c[0], c_ctx])[:, :, None], (2, d, LANES))
    return pl.pallas_call(
        functools.partial(_mod_kernel, tn=tn),
        grid=(n // tn, d // tk),
        in_specs=[pl.BlockSpec((2, tk, LANES), lambda j, k: (0, k, 0)),
                  pl.BlockSpec((tk, tn), lambda j, k: (k, j)),
                  pl.BlockSpec((1, tn), lambda j, k: (0, j))],
        out_specs=pl.BlockSpec((SUBLANES, tn), lambda j, k: (0, j)),
        out_shape=jax.ShapeDtypeStruct((SUBLANES, n), F32),
        compiler_params=_params(2),
        name="modulation",
    )(cb, w_mod, b_mod.reshape(1, n))


def _premod_kernel(x_ref, sh_ref, sc_ref, o_ref, *, row):
    scale = 1.0 + sc_ref[row:row + 1, :]
    shift = sh_ref[row:row + 1, :]
    for r0 in range(0, x_ref.shape[0], LN_STRIP):
        rows = slice(r0, r0 + LN_STRIP)
        o_ref[rows, :] = (_ln_rows(x_ref[rows, :], ADALN_EPS) * scale + shift).astype(BF16)


def _premod(x2d, mods, row):
    m, d = x2d.shape
    tm = _pick(m, (256, 128))
    return pl.pallas_call(
        functools.partial(_premod_kernel, row=row),
        grid=(m // tm,),
        in_specs=[pl.BlockSpec((tm, d), lambda i: (i, 0)),
                  pl.BlockSpec((SUBLANES, d), lambda i: (0, 0)),
                  pl.BlockSpec((SUBLANES, d), lambda i: (0, 1))],
        out_specs=pl.BlockSpec((tm, d), lambda i: (i, 0)),
        out_shape=jax.ShapeDtypeStruct((m, d), BF16),
        compiler_params=_params(1),
        name="premod",
    )(x2d, mods, mods)


def _proj_kernel(*refs, rope, scale, tn):
    if rope:
        h_ref, w_ref, cos_ref, sa_ref, sb_ref, o_ref, wbf = refs
    else:
        h_ref, w_ref, o_ref, wbf = refs

    @pl.when(pl.program_id(1) == 0)
    def _():
        wbf[...] = w_ref[...].astype(BF16)

    if rope:
        half = h_ref.shape[0] // ROPE_ROW_PARTS
        for r0 in range(0, h_ref.shape[0], half):
            rows = slice(r0, r0 + half)
            acc = jnp.dot(h_ref[rows, :], wbf[...], preferred_element_type=F32)
            cos, sa, sb = cos_ref[rows, :], sa_ref[rows, :], sb_ref[rows, :]
            for c in range(tn // LANES):
                cols = slice(c * LANES, (c + 1) * LANES)
                a = acc[:, cols]
                r = a * cos + pltpu.roll(a, LANES - 16, 1) * sa + pltpu.roll(a, 16, 1) * sb
                o_ref[rows, cols] = (r * scale).astype(o_ref.dtype)
    else:
        o_ref[...] = jnp.dot(h_ref[...], wbf[...], preferred_element_type=F32).astype(o_ref.dtype)


def _project(h, w, col_off, n_cols, out_dtype, rope_tabs=None, scale=1.0):
    m, k = h.shape
    tn = _pick(math.gcd(n_cols, col_off), (512, 256, 128))
    tm = _pick(m, (1024, 512, 256, 128))
    off = col_off // tn
    in_specs = [pl.BlockSpec((tm, k), lambda j, i: (i, 0)),
                pl.BlockSpec((k, tn), lambda j, i: (0, j + off))]
    args = [h, w]
    if rope_tabs is not None:
        in_specs += [pl.BlockSpec((tm, LANES), lambda j, i: (i, 0))] * 3
        args += list(rope_tabs)
    return pl.pallas_call(
        functools.partial(_proj_kernel, rope=rope_tabs is not None, scale=scale, tn=tn),
        grid=(n_cols // tn, m // tm),
        in_specs=in_specs,
        out_specs=pl.BlockSpec((tm, tn), lambda j, i: (i, j)),
        out_shape=jax.ShapeDtypeStruct((m, n_cols), out_dtype),
        scratch_shapes=[pltpu.VMEM((k, tn), BF16)],
        compiler_params=_params(2),
        name="project",
    )(*args)


def _rope_tables(n_tokens):
    rows = n_tokens // GRID_W
    row = jnp.broadcast_to(jnp.arange(rows, dtype=F32)[:, None], (rows, GRID_W)).reshape(-1)
    col = jnp.broadcast_to(jnp.arange(GRID_W, dtype=F32)[None, :], (rows, GRID_W)).reshape(-1)
    axis_dim = HEAD_DIM // 2
    nf = axis_dim // 2
    inv_freq = ROPE_BASE ** (-jnp.arange(0, axis_dim, 2, dtype=F32) / axis_dim)
    ang = jnp.stack([row[:, None] * inv_freq, col[:, None] * inv_freq], axis=1)
    cos, sin = jnp.cos(ang), jnp.sin(ang)
    zero = jnp.zeros_like(sin)
    cos64 = jnp.concatenate([cos, cos], axis=-1).reshape(n_tokens, HEAD_DIM)
    sa64 = jnp.concatenate([-sin, zero], axis=-1).reshape(n_tokens, HEAD_DIM)
    sb64 = jnp.concatenate([zero, sin], axis=-1).reshape(n_tokens, HEAD_DIM)
    rep = LANES // HEAD_DIM
    return tuple(jnp.tile(t, (1, rep)) for t in (cos64, sa64, sb64))


def _attn_kernel(q_ref, k_ref, v_ref, kc_ref, vc_ref, dl_ref, g_ref, o_ref, qs_sc, vx_sc, sa_sc, sb_sc, sc_sc,
                 pa_sc, pb_sc, xa_sc, xb_sc, xc_sc, m_sc, acc_sc, *, tq, tk, n_chunks):
    s_len = n_chunks * tk
    c_len = kc_ref.shape[0]

    @pl.when(pl.program_id(1) == 0)
    def _():
        vx_sc[0:s_len, 0:V_DIM] = v_ref[...]
        vx_sc[s_len:s_len + c_len, 0:V_DIM] = vc_ref[...]
        vx_sc[:, V_DIM:2 * V_DIM] = jnp.ones((s_len + c_len, V_DIM), BF16)

    q = q_ref[...]
    lane = lax.broadcasted_iota(I32, q.shape, 1)
    zero = jnp.zeros_like(q)
    qs_sc[0:tq, :] = jnp.where(lane < HEAD_DIM, q, zero)
    qs_sc[tq:2 * tq, :] = jnp.where(lane >= HEAD_DIM, q, zero)
    m_sc[...] = jnp.full_like(m_sc, NEG_BIG)
    acc_sc[...] = jnp.zeros_like(acc_sc)

    def scores(s_ref, x_ref, kb):
        s = lax.dot_general(qs_sc[...], kb, (((1,), (1,)), ((), ())), preferred_element_type=F32)
        s_ref[...] = s
        x_ref[...] = jnp.broadcast_to(jnp.max(s, axis=-1, keepdims=True), x_ref.shape)

    def absorb(s_ref, x_ref, p_ref, v0):
        width = s_ref.shape[1]
        for r0 in range(0, 2 * tq, ATTN_STRIP):
            rows = slice(r0, r0 + ATTN_STRIP)
            m_prev = m_sc[rows, :]
            m_new = jnp.maximum(m_prev, x_ref[rows, :])
            x_ref[rows, :] = jnp.exp2(m_prev - m_new)
            m_sc[rows, :] = m_new
            p_ref[rows, 0:width] = jnp.exp2((s_ref[rows, :] - jnp.tile(m_new, (1, width // LANES))).astype(BF16))
        pv = jnp.dot(p_ref[:, 0:width], vx_sc[pl.ds(v0, width), :], preferred_element_type=F32)
        acc_sc[...] = jnp.tile(x_ref[...], (1, 2)) * acc_sc[...] + pv

    def kchunk(c):
        return k_ref[pl.ds(pl.multiple_of(c * tk, tk), tk), :]

    scores(sa_sc, xa_sc, kchunk(0))

    def body(j, carry):
        c = 2 * j
        scores(sb_sc, xb_sc, kchunk(c + 1))
        absorb(sa_sc, xa_sc, pa_sc, pl.multiple_of(c * tk, tk))
        scores(sa_sc, xa_sc, kchunk(c + 2))
        absorb(sb_sc, xb_sc, pb_sc, pl.multiple_of((c + 1) * tk, tk))
        return carry

    lax.fori_loop(0, (n_chunks - 2) // 2, body, 0)
    scores(sb_sc, xb_sc, kchunk(n_chunks - 1))
    absorb(sa_sc, xa_sc, pa_sc, (n_chunks - 2) * tk)
    scores(sc_sc, xc_sc, kc_ref[...])
    absorb(sb_sc, xb_sc, pb_sc, (n_chunks - 1) * tk)
    absorb(sc_sc, xc_sc, pa_sc, s_len)

    dl = dl_ref[...]
    lam = (jnp.exp(jnp.sum(dl[0:1] * dl[1:2], axis=-1, keepdims=True))
           - jnp.exp(jnp.sum(dl[2:3] * dl[3:4], axis=-1, keepdims=True)) + LAMBDA_INIT)
    o = (acc_sc[0:tq, 0:V_DIM] / acc_sc[0:tq, V_DIM:2 * V_DIM]
         - lam * (acc_sc[tq:2 * tq, 0:V_DIM] / acc_sc[tq:2 * tq, V_DIM:2 * V_DIM]))
    o = o * lax.rsqrt(jnp.mean(o * o, axis=-1, keepdims=True) + SUBLN_EPS)
    o_ref[...] = (o * g_ref[...] * (1.0 - LAMBDA_INIT)).astype(o_ref.dtype)


def _attention(q, k, v, kc, vc, diff_lambda, subln_g):
    s_len, width = q.shape
    c_len = kc.shape[0]
    tq = _pick(s_len, (512, 256, 128))
    tk = _pick(s_len // 2, (2048, 1024, 512, 256, 128))
    n_chunks = s_len // tk
    assert n_chunks % 2 == 0
    n_heads = width // V_DIM
    return pl.pallas_call(
        functools.partial(_attn_kernel, tq=tq, tk=tk, n_chunks=n_chunks),
        grid=(n_heads, s_len // tq),
        in_specs=[pl.BlockSpec((tq, V_DIM), lambda h, i: (i, h)),
                  pl.BlockSpec((s_len, V_DIM), lambda h, i: (0, h)),
                  pl.BlockSpec((s_len, V_DIM), lambda h, i: (0, h)),
                  pl.BlockSpec((c_len, V_DIM), lambda h, i: (0, h)),
                  pl.BlockSpec((c_len, V_DIM), lambda h, i: (0, h)),
                  pl.BlockSpec((4, HEAD_DIM), lambda h, i: (0, 0)),
                  pl.BlockSpec((1, V_DIM), lambda h, i: (0, 0))],
        out_specs=pl.BlockSpec((tq, V_DIM), lambda h, i: (i, h)),
        out_shape=jax.ShapeDtypeStruct((s_len, width), BF16),
        scratch_shapes=[pltpu.VMEM((2 * tq, V_DIM), BF16), pltpu.VMEM((s_len + c_len, 2 * V_DIM), BF16),
                        pltpu.VMEM((2 * tq, tk), F32), pltpu.VMEM((2 * tq, tk), F32),
                        pltpu.VMEM((2 * tq, c_len), F32)]
                       + [pltpu.VMEM((2 * tq, tk), BF16)] * 2
                       + [pltpu.VMEM((2 * tq, LANES), F32)] * 4
                       + [pltpu.VMEM((2 * tq, 2 * V_DIM), F32)],
        compiler_params=_params(2),
        name="diff_attention",
    )(q, k, v, kc, vc, diff_lambda, subln_g.reshape(1, V_DIM))


def _pool_kernel(p_ref, prev_ref, next_ref, w_ref, sc_ref, o_ref, ext, *, tm, n_tokens, gdim):
    i = pl.program_id(0)
    last = pl.num_programs(0) - 1
    ext[0:POOL_HALO, :] = jnp.where(i > 0, prev_ref[...], 0.0)
    ext[POOL_HALO:POOL_HALO + tm, :] = p_ref[...]
    ext[POOL_HALO + tm:2 * POOL_HALO + tm, :] = jnp.where(i < last, next_ref[...], 0.0)
    t = i * tm + lax.broadcasted_iota(I32, (tm, 1), 0)
    for g, w in enumerate(POOL_WINDOWS):
        cols = slice(g * gdim, (g + 1) * gdim)
        ssum = None
        for d in range(-(w // 2), w // 2):
            part = ext[POOL_HALO + d:POOL_HALO + d + tm, cols]
            ssum = part if ssum is None else ssum + part
        cnt = jnp.minimum(t + w // 2, n_tokens) - jnp.maximum(t - w // 2, 0)
        mean = ssum / cnt.astype(F32)
        mg = (mean - p_ref[:, cols]).astype(BF16)
        y = jnp.dot(mg, w_ref[g].astype(BF16), preferred_element_type=F32)
        o_ref[:, cols] = (y * sc_ref[:, cols]).astype(o_ref.dtype)


def _pool(p, w_pool, pool_scale):
    s_len, width = p.shape
    n_groups, gdim, _ = w_pool.shape
    assert n_groups == len(POOL_WINDOWS) and max(POOL_WINDOWS) // 2 <= POOL_HALO
    tm = _pick(s_len, (512, 256, 128))
    hb = tm // POOL_HALO
    n_halo_blocks = s_len // POOL_HALO
    return pl.pallas_call(
        functools.partial(_pool_kernel, tm=tm, n_tokens=s_len, gdim=gdim),
        grid=(s_len // tm,),
        in_specs=[pl.BlockSpec((tm, width), lambda i: (i, 0)),
                  pl.BlockSpec((POOL_HALO, width), lambda i: (jnp.maximum(i * hb - 1, 0), 0)),
                  pl.BlockSpec((POOL_HALO, width), lambda i: (jnp.minimum((i + 1) * hb, n_halo_blocks - 1), 0)),
                  pl.BlockSpec((n_groups, gdim, gdim), lambda i: (0, 0, 0)),
                  pl.BlockSpec((1, width), lambda i: (0, 0))],
        out_specs=pl.BlockSpec((tm, width), lambda i: (i, 0)),
        out_shape=jax.ShapeDtypeStruct((s_len, width), BF16),
        scratch_shapes=[pltpu.VMEM((tm + 2 * POOL_HALO, width), F32)],
        compiler_params=_params(1),
        name="pool_mixer",
    )(p, p, p, w_pool, pool_scale.reshape(1, width))


def _merge_kernel(a_ref, p_ref, wa_ref, wb_ref, ga_ref, gb_ref, o_ref, wabf, wbbf):
    @pl.when(pl.program_id(1) == 0)
    def _():
        wabf[...] = wa_ref[...].astype(BF16)
        wbbf[...] = wb_ref[...].astype(BF16)

    part = a_ref.shape[0] // MERGE_ROW_PARTS
    for r0 in range(0, a_ref.shape[0], part):
        rows = slice(r0, r0 + part)
        ya = jnp.dot(a_ref[rows, :], wabf[...], preferred_element_type=F32)
        yb = jnp.dot(p_ref[rows, :], wbbf[...], preferred_element_type=F32)
        o_ref[rows, :] = (jax.nn.sigmoid(ga_ref[rows, :]) * ya
                          + jax.nn.sigmoid(gb_ref[rows, :]) * yb).astype(o_ref.dtype)


def _merge(attn, pooled, w_a, w_b, gates):
    m, ka = attn.shape
    kb = pooled.shape[1]
    n = w_a.shape[1]
    tn = _pick(n, (512, 256, 128))
    tm = _pick(m, (1024, 512, 256, 128))
    gb_off = n // tn
    return pl.pallas_call(
        _merge_kernel,
        grid=(n // tn, m // tm),
        in_specs=[pl.BlockSpec((tm, ka), lambda j, i: (i, 0)),
                  pl.BlockSpec((tm, kb), lambda j, i: (i, 0)),
                  pl.BlockSpec((ka, tn), lambda j, i: (0, j)),
                  pl.BlockSpec((kb, tn), lambda j, i: (0, j)),
                  pl.BlockSpec((tm, tn), lambda j, i: (i, j)),
                  pl.BlockSpec((tm, tn), lambda j, i: (i, j + gb_off))],
        out_specs=pl.BlockSpec((tm, tn), lambda j, i: (i, j)),
        out_shape=jax.ShapeDtypeStruct((m, n), BF16),
        scratch_shapes=[pltpu.VMEM((ka, tn), BF16), pltpu.VMEM((kb, tn), BF16)],
        compiler_params=_params(2),
        name="gated_merge",
    )(attn, pooled, w_a, w_b, gates, gates)


def _post1_kernel(x_ref, y_ref, g1_ref, sh_ref, sc_ref, lg_ref, lb_ref, wr_ref, x1_ref, hp_ref, aff_ref, *, half):
    z = DEEPNORM_ALPHA * x_ref[...] + g1_ref[0:1, :] * y_ref[...]
    x1 = _ln_rows(z, LN_EPS) * lg_ref[...] + lb_ref[...]
    x1_ref[...] = x1
    h2 = _ln_rows(x1, ADALN_EPS) * (1.0 + sc_ref[0:1, :]) + sh_ref[0:1, :]
    hb = lax.bitcast_convert_type(h2.astype(BF16).astype(F32), U32)
    hp_ref[...] = (hb[:, :half] >> 16) | (hb[:, half:] & jnp.uint32(0xFFFF0000))
    logits = lax.dot_general(wr_ref[...], h2, (((1,), (1,)), ((), ())),
                             precision=lax.Precision.HIGHEST, preferred_element_type=F32)
    mx = jnp.max(logits, axis=0, keepdims=True)
    ex = jnp.exp(logits - mx)
    aff_ref[...] = ex / jnp.sum(ex, axis=0, keepdims=True)


def _post1(x2d, y, mods, ln_g, ln_b, w_router_t):
    m, d = x2d.shape
    n_exp = w_router_t.shape[0]
    tm = _pick(m, (256, 128))
    row = lambda i: (i, 0)
    vec = lambda i: (0, 0)
    return pl.pallas_call(
        functools.partial(_post1_kernel, half=d // 2),
        grid=(m // tm,),
        in_specs=[pl.BlockSpec((tm, d), row), pl.BlockSpec((tm, d), row),
                  pl.BlockSpec((SUBLANES, d), lambda i: (0, 2)),
                  pl.BlockSpec((SUBLANES, d), lambda i: (0, 3)),
                  pl.BlockSpec((SUBLANES, d), lambda i: (0, 4)),
                  pl.BlockSpec((1, d), vec), pl.BlockSpec((1, d), vec),
                  pl.BlockSpec((n_exp, d), vec)],
        out_specs=[pl.BlockSpec((tm, d), row), pl.BlockSpec((tm, d // 2), row),
                   pl.BlockSpec((n_exp, tm), lambda i: (0, i))],
        out_shape=[jax.ShapeDtypeStruct((m, d), F32), jax.ShapeDtypeStruct((m, d // 2), U32),
                   jax.ShapeDtypeStruct((n_exp, m), F32)],
        compiler_params=_params(1),
        name="ln1_router",
    )(x2d, y, mods, mods, mods, ln_g.reshape(1, d), ln_b.reshape(1, d), w_router_t)


def _route_kernel(aff_ref, idx_ref, csel_sc, *, n_exp, nc, cap):
    rows = n_exp * nc
    aff3 = aff_ref[...]
    ri = lax.broadcasted_iota(I32, (rows, rows), 0)
    rj = lax.broadcasted_iota(I32, (rows, rows), 1)
    before = jnp.where(((ri // nc) == (rj // nc)) & (rj < ri), 1.0, 0.0).astype(BF16)
    li = lax.broadcasted_iota(I32, (LANES, LANES), 0)
    lj = lax.broadcasted_iota(I32, (LANES, LANES), 1)
    tri = jnp.where(li <= lj, 1.0, 0.0).astype(BF16)
    ones = jnp.ones((LANES, LANES), BF16)

    def per_expert_sum(x3):
        return jnp.sum(jnp.sum(x3, axis=1, keepdims=True), axis=2, keepdims=True)

    def count(mask3):
        return per_expert_sum(jnp.where(mask3, 1.0, 0.0))

    def prefix(mask2):
        mb = jnp.where(mask2, 1.0, 0.0).astype(BF16)
        within = jnp.dot(mb, tri, preferred_element_type=F32)
        row_tot = jnp.dot(mb, ones, preferred_element_type=F32).astype(BF16)
        return within + jnp.dot(before, row_tot, preferred_element_type=F32)

    def rows_of(x3):
        return jnp.broadcast_to(x3, (n_exp, nc, LANES)).reshape(rows, LANES)

    top = jnp.max(jnp.max(aff3, axis=1, keepdims=True), axis=2, keepdims=True)
    lo0 = jnp.zeros_like(top)
    hi0 = 2.0 * top + 1e-30

    def bisect(_, carry):
        lo, hi = carry
        mid = 0.5 * (lo + hi)
        ok = count(aff3 >= mid) >= cap
        return jnp.where(ok, mid, lo), jnp.where(ok, hi, mid)

    thr3, _ = lax.fori_loop(0, BISECT_STEPS, bisect, (lo0, hi0))
    need = rows_of(cap - count(aff3 > thr3))
    aff2 = aff3.reshape(rows, LANES)
    thr2 = rows_of(thr3)
    gt = aff2 > thr2
    eq = aff2 == thr2
    sel = gt | (eq & (prefix(eq) <= need))
    csel_sc[...] = prefix(sel)

    lane = lax.broadcasted_iota(I32, (1, LANES), 1)
    lane_col = lax.broadcasted_iota(I32, (cap, LANES), 1)
    slot = lax.broadcasted_iota(I32, (cap, LANES), 0).astype(F32) + 1.0
    pad = jnp.zeros((LANES - nc, LANES), F32)

    def per_expert(e, carry):
        counts = jnp.concatenate([csel_sc[pl.ds(pl.multiple_of(e * nc, SUBLANES), nc), :], pad], axis=0)
        ends = jnp.where(lane < nc, counts.T[LANES - 1:LANES, :], float(2 * cap))
        chunk = jnp.sum(jnp.where(ends < slot, 1.0, 0.0), axis=-1, keepdims=True)
        pick = jnp.where(lane_col == chunk.astype(I32), 1.0, 0.0)
        row = jnp.dot(pick, counts, precision=lax.Precision.HIGHEST, preferred_element_type=F32)
        within = jnp.sum(jnp.where(row < slot, 1.0, 0.0), axis=-1, keepdims=True)
        idx_ref[e] = (chunk * float(LANES) + within).astype(I32)
        return carry

    lax.fori_loop(0, n_exp, per_expert, 0)


def _route(aff_t, cap):
    n_exp, s_len = aff_t.shape
    nc = s_len // LANES
    rows = n_exp * nc
    assert nc <= LANES and nc % SUBLANES == 0
    return pl.pallas_call(
        functools.partial(_route_kernel, n_exp=n_exp, nc=nc, cap=cap),
        grid=(1,),
        in_specs=[pl.BlockSpec((n_exp, nc, LANES), lambda i: (0, 0, 0))],
        out_specs=pl.BlockSpec((n_exp, cap, 1), lambda i: (0, 0, 0)),
        out_shape=jax.ShapeDtypeStruct((n_exp, cap, 1), I32),
        scratch_shapes=[pltpu.VMEM((rows, LANES), F32)],
        compiler_params=_params(1),
        name="expert_choice",
    )(aff_t.reshape(n_exp, nc, LANES))


def _ffn_kernel(idx_ref, hp_hbm, aff_hbm, wg_ref, wu_ref, wd_ref, o_ref, hsel, gsel, hb, gate_sc, act, sem,
                *, n_exp, cap, nf, nn, tf, half):
    e = pl.program_id(0)
    s = pl.program_id(1)
    n_steps = nf + nn
    rows_per_step = cap // nn
    nxt = jnp.minimum(e + 1, n_exp - 1)

    def start_row(ee, r):
        t = idx_ref[ee * cap + r]
        pltpu.make_async_copy(hp_hbm.at[pl.ds(t, 1), :], hsel.at[pl.ds(r, 1), :], sem.at[0]).start()
        pltpu.make_async_copy(aff_hbm.at[pl.ds(t, 1), :], gsel.at[pl.ds(r, 1), :], sem.at[1]).start()

    def wait_gather():
        pltpu.make_async_copy(hp_hbm.at[pl.ds(0, cap), :], hsel, sem.at[0]).wait()
        pltpu.make_async_copy(aff_hbm.at[pl.ds(0, cap), :], gsel, sem.at[1]).wait()

    def gather_slice():
        for r in range(rows_per_step):
            start_row(nxt, (s - nf) * rows_per_step + r)

    @pl.when((e == 0) & (s == 0))
    def _():
        def go(r, carry):
            start_row(0, r)
            return carry
        lax.fori_loop(0, cap, go, 0, unroll=GATHER_UNROLL)

    @pl.when(s == 0)
    def _():
        wait_gather()
        u = hsel[...]
        hb[:, :half] = lax.bitcast_convert_type(u << 16, F32).astype(BF16)
        hb[:, half:] = lax.bitcast_convert_type(u & jnp.uint32(0xFFFF0000), F32).astype(BF16)
        g = gsel[...]
        mine = jnp.where(lax.broadcasted_iota(I32, g.shape, 1) == e, g, 0.0)
        gate_sc[...] = jnp.broadcast_to(jnp.sum(mine, axis=-1, keepdims=True), g.shape)

    @pl.when(s < nf)
    def _():
        x = hb[...]
        u = jnp.dot(x, wg_ref[0].astype(BF16), preferred_element_type=F32)
        v = jnp.dot(x, wu_ref[0].astype(BF16), preferred_element_type=F32)
        act[s] = (u * jax.nn.sigmoid(u) * v).astype(BF16)

    @pl.when(s >= nf)
    def _():
        gather_slice()
        o = jnp.dot(act[0], wd_ref[0, 0:tf, :].astype(BF16), preferred_element_type=F32)
        for f in range(1, nf):
            o += jnp.dot(act[f], wd_ref[0, f * tf:(f + 1) * tf, :].astype(BF16), preferred_element_type=F32)
        o_ref[...] = (o * jnp.tile(gate_sc[...], (1, o.shape[1] // LANES))).astype(o_ref.dtype)

    @pl.when((e == n_exp - 1) & (s == n_steps - 1))
    def _():
        wait_gather()


def _expert_ffn(idx_flat, hp, aff_pad, w_gate, w_up, w_down):
    n_exp, d, ff = w_gate.shape
    cap = idx_flat.shape[0] // n_exp
    half = hp.shape[1]
    tf = _pick(ff, (256, 128))
    tn = _pick(d, (512, 256, 128))
    nf, nn = ff // tf, d // tn
    assert cap % nn == 0
    last_e = n_exp - 1

    def up_map(e, s, idx):
        return (jnp.where(s < nf, e, jnp.minimum(e + 1, last_e)), 0, jnp.where(s < nf, s, 0))

    def down_map(e, s, idx):
        return (e, 0, jnp.maximum(s - nf, 0))

    def out_map(e, s, idx):
        return (e, jnp.maximum(s - nf, 0))

    grid_spec = pltpu.PrefetchScalarGridSpec(
        num_scalar_prefetch=1,
        grid=(n_exp, nf + nn),
        in_specs=[pl.BlockSpec(memory_space=pl.ANY),
                  pl.BlockSpec(memory_space=pl.ANY),
                  pl.BlockSpec((1, d, tf), up_map),
                  pl.BlockSpec((1, d, tf), up_map),
                  pl.BlockSpec((1, ff, tn), down_map)],
        out_specs=pl.BlockSpec((cap, tn), out_map),
        scratch_shapes=[pltpu.VMEM((cap, half), U32), pltpu.VMEM((cap, LANES), F32),
                        pltpu.VMEM((cap, d), BF16), pltpu.VMEM((cap, LANES), F32),
                        pltpu.VMEM((nf, cap, tf), BF16), pltpu.SemaphoreType.DMA((2,))],
    )
    return pl.pallas_call(
        functools.partial(_ffn_kernel, n_exp=n_exp, cap=cap, nf=nf, nn=nn, tf=tf, half=half),
        grid_spec=grid_spec,
        out_shape=jax.ShapeDtypeStruct((n_exp * cap, d), BF16),
        compiler_params=_params(2),
        name="expert_ffn",
    )(idx_flat, hp, aff_pad, w_gate, w_up, w_down)


def _combine_kernel(off_ref, tot_ref, o_hbm, idx_hbm, x1_ref, g2_ref, lg_ref, lb_ref, out_ref, rbuf, ibuf, acc, sem,
                    *, tb, kbuf, max_pieces, n_blocks):
    b = pl.program_id(0)
    slot = b % 2
    pieces_per_round = kbuf // PIECE

    @pl.when(b == 0)
    def _():
        rbuf[...] = jnp.zeros_like(rbuf)
        ibuf[...] = jnp.full_like(ibuf, -1)

    def copies(blk, sl, j_global, j_local):
        src = pl.multiple_of(off_ref[blk * max_pieces + j_global], PIECE)
        dst = pl.multiple_of(j_local * PIECE, PIECE)
        return (pltpu.make_async_copy(o_hbm.at[pl.ds(src, PIECE), :], rbuf.at[sl, pl.ds(dst, PIECE), :],
                                      sem.at[0, sl]),
                pltpu.make_async_copy(idx_hbm.at[pl.ds(src, PIECE), :], ibuf.at[sl, pl.ds(dst, PIECE), :],
                                      sem.at[1, sl]))

    def pieces_in_round(blk, r):
        return jnp.clip(tot_ref[blk] - r * pieces_per_round, 0, pieces_per_round)

    def start_round(blk, sl, r):
        def go(j, c):
            for cp in copies(blk, sl, r * pieces_per_round + j, j):
                cp.start()
            return c
        lax.fori_loop(0, pieces_in_round(blk, r), go, 0)

    def wait_round(blk, sl, r):
        def go(j, c):
            for cp in copies(blk, sl, r * pieces_per_round + j, j):
                cp.wait()
            return c
        lax.fori_loop(0, pieces_in_round(blk, r), go, 0)

    @pl.when(b == 0)
    def _():
        start_round(0, 0, 0)

    @pl.when(b + 1 < n_blocks)
    def _():
        start_round(b + 1, 1 - slot, 0)

    acc[...] = jnp.zeros_like(acc)
    tok = b * tb + lax.broadcasted_iota(I32, (COMBINE_GROUP, tb), 1)

    def one_round(r, carry):
        @pl.when(r > 0)
        def _():
            start_round(b, slot, r)

        wait_round(b, slot, r)
        rows = pieces_in_round(b, r) * PIECE

        def group(g, c):
            r0 = pl.multiple_of(g * COMBINE_GROUP, COMBINE_GROUP)
            k = r0 + lax.broadcasted_iota(I32, (COMBINE_GROUP, tb), 0)
            ids = jnp.tile(ibuf[slot, pl.ds(r0, COMBINE_GROUP), :], (1, tb // LANES))
            onehot = jnp.where((ids == tok) & (k < rows), 1.0, 0.0).astype(BF16)
            acc[...] += lax.dot_general(onehot, rbuf[slot, pl.ds(r0, COMBINE_GROUP), :], (((0,), (0,)), ((), ())),
                                        preferred_element_type=F32)
            return c

        lax.fori_loop(0, (rows + COMBINE_GROUP - 1) // COMBINE_GROUP, group, 0)
        return carry

    lax.fori_loop(0, (tot_ref[b] + pieces_per_round - 1) // pieces_per_round, one_round, 0)
    z = DEEPNORM_ALPHA * x1_ref[...] + g2_ref[0:1, :] * acc[...]
    out_ref[...] = _ln_rows(z, LN_EPS) * lg_ref[...] + lb_ref[...]


def _combine(expert_out, idx, x1, mods, ln_g, ln_b):
    n_exp, cap = idx.shape
    s_len, d = x1.shape
    tb = _pick(s_len, (256, 128))
    nb = s_len // tb
    max_pieces = n_exp * (cap // PIECE)
    kbuf = min(COMBINE_ROWS, max_pieces * PIECE)

    edges = jnp.arange(nb + 1, dtype=I32) * tb
    below = jnp.sum((idx[None, :, :] < edges[:, None, None]).astype(I32), axis=2)
    s0, s1 = below[:-1], below[1:]
    p0 = s0 // PIECE
    n_pc = jnp.where(s1 > s0, (s1 + PIECE - 1) // PIECE - p0, 0)
    cum = jnp.cumsum(n_pc, axis=1)
    total = cum[:, -1]
    j = jnp.arange(max_pieces, dtype=I32)[None, :]
    e_of = jnp.minimum(jnp.sum((j[:, :, None] >= cum[:, None, :]).astype(I32), axis=2), n_exp - 1)
    mine = e_of[:, :, None] == jnp.arange(n_exp, dtype=I32)[None, None, :]
    first_piece = jnp.sum(jnp.where(mine, (p0 - (cum - n_pc))[:, None, :], 0), axis=2)
    off = jnp.where(j < total[:, None], e_of * cap + (first_piece + j) * PIECE, 0).astype(I32)

    idx_rep = jnp.broadcast_to(idx.reshape(n_exp * cap, 1), (n_exp * cap, LANES))
    grid_spec = pltpu.PrefetchScalarGridSpec(
        num_scalar_prefetch=2,
        grid=(nb,),
        in_specs=[pl.BlockSpec(memory_space=pl.ANY),
                  pl.BlockSpec(memory_space=pl.ANY),
                  pl.BlockSpec((tb, d), lambda i, off, tot: (i, 0)),
                  pl.BlockSpec((SUBLANES, d), lambda i, off, tot: (0, 5)),
                  pl.BlockSpec((1, d), lambda i, off, tot: (0, 0)),
                  pl.BlockSpec((1, d), lambda i, off, tot: (0, 0))],
        out_specs=pl.BlockSpec((tb, d), lambda i, off, tot: (i, 0)),
        scratch_shapes=[pltpu.VMEM((2, kbuf, d), BF16), pltpu.VMEM((2, kbuf, LANES), I32),
                        pltpu.VMEM((tb, d), F32), pltpu.SemaphoreType.DMA((2, 2))],
    )
    return pl.pallas_call(
        functools.partial(_combine_kernel, tb=tb, kbuf=kbuf, max_pieces=max_pieces, n_blocks=nb),
        grid_spec=grid_spec,
        out_shape=jax.ShapeDtypeStruct((s_len, d), F32),
        compiler_params=_params(1),
        name="combine_ln2",
    )(off.reshape(-1), total.astype(I32), expert_out, idx_rep, x1, mods, ln_g.reshape(1, d), ln_b.reshape(1, d))


def kernel(x, c, ctx, c_ctx, w_mod, b_mod, w_in, diff_lambda, diff_subln_g, w_pool, pool_scale, w_branch_attn,
           w_branch_pool, w_out, ln1_g, ln1_b, w_router, w_gate, w_up, w_down, ln2_g, ln2_b):
    batch, s_len, d = x.shape
    assert batch == 1 and w_mod.shape[0] == DEPTH
    attn_w = N_HEADS * V_DIM
    pool_w = w_pool.shape[1] * w_pool.shape[2]
    k_off, v_off, p_off = attn_w, 2 * attn_w, 3 * attn_w
    g_off = p_off + pool_w
    n_exp = w_router.shape[-1]
    cap = EC_CAPACITY_FACTOR * s_len // n_exp

    mods = _modulation(c, c_ctx, w_mod[0], b_mod[0])
    h = _premod(x[0], mods, 0)
    hc = _premod(ctx[0], mods, 1)

    w = w_in[0]
    tabs = _rope_tables(s_len)
    q = _project(h, w, 0, attn_w, BF16, tabs, HEAD_DIM ** -0.5 * math.log2(math.e))
    k = _project(h, w, k_off, attn_w, BF16, tabs)
    v = _project(h, w, v_off, attn_w, BF16)
    kc = _project(hc, w, k_off, attn_w, BF16)
    vc = _project(hc, w, v_off, attn_w, BF16)
    p = _project(h, w, p_off, pool_w, F32)
    gates = _project(h, w, g_off, 2 * d, F32)

    attn = _attention(q, k, v, kc, vc, diff_lambda[0], diff_subln_g[0])
    pooled = _pool(p, w_pool[0], pool_scale[0])
    merged = _merge(attn, pooled, w_branch_attn[0], w_branch_pool[0], gates)
    y = _project(merged, w_out[0], 0, d, F32)

    x1, hp, aff_t = _post1(x[0], y, mods, ln1_g[0], ln1_b[0], w_router[0].T)
    idx = _route(aff_t, cap).reshape(n_exp, cap)
    aff_pad = jnp.pad(aff_t.T, ((0, 0), (0, LANES - n_exp)))
    expert_out = _expert_ffn(idx.reshape(-1), hp, aff_pad, w_gate[0], w_up[0], w_down[0])
    out = _combine(expert_out, idx, x1, mods, ln2_g[0], ln2_b[0])
    return out.reshape(batch, s_len, d)
```

```python
import functools
import math

import jax
import jax.numpy as jnp
from jax import lax
from jax.experimental import pallas as pl
from jax.experimental.pallas import tpu as pltpu

F32 = jnp.float32
BF16 = jnp.bfloat16
I32 = jnp.int32
U32 = jnp.uint32

LANES = 128
SUBLANES = 8
VMEM_LIMIT = 56 * 1024 * 1024

GRID_W = 64
N_HEADS = 16
HEAD_DIM = 64
V_DIM = 2 * HEAD_DIM
POOL_WINDOWS = (2, 4, 8, 16)
POOL_HALO = 8
EC_CAPACITY_FACTOR = 2
ROPE_BASE = 10000.0
LN_EPS = 1e-5
ADALN_EPS = 1e-6
SUBLN_EPS = 1e-5
DEPTH = 1
DEEPNORM_ALPHA = (2.0 * DEPTH) ** 0.25
LAMBDA_INIT = 0.8 - 0.6 * math.exp(-0.3 * 0)
NEG_BIG = -1e30
BISECT_STEPS = 64
GATHER_UNROLL = 8
ROPE_ROW_PARTS = 8
MERGE_ROW_PARTS = 4
PIECE = 16
COMBINE_GROUP = 256
COMBINE_ROWS = 1024
LN_STRIP = 16
ATTN_STRIP = 32


def _pick(n, prefs):
    for p in prefs:
        if n % p == 0:
            return p
    raise ValueError(f"no tile for {n} in {prefs}")


def _params(n_axes):
    return pltpu.CompilerParams(dimension_semantics=("arbitrary",) * n_axes,
                                vmem_limit_bytes=VMEM_LIMIT)


def _ln_rows(x, eps):
    mu = jnp.mean(x, axis=-1, keepdims=True)
    xc = x - mu
    var = jnp.mean(xc * xc, axis=-1, keepdims=True)
    return xc * lax.rsqrt(var + eps)


def _mod_kernel(cb_ref, w_ref, b_ref, o_ref, *, tn):
    @pl.when(pl.program_id(1) == 0)
    def _():
        o_ref[...] = jnp.broadcast_to(b_ref[...], o_ref.shape)

    w = w_ref[...]
    for i in range(2):
        cv = cb_ref[i]
        s = jnp.tile(cv * jax.nn.sigmoid(cv), (1, tn // LANES))
        o_ref[i:i + 1, :] += jnp.sum(w * s, axis=0, keepdims=True)


def _modulation(c, c_ctx, w_mod, b_mod):
    d, n = w_mod.shape
    tn = _pick(n, (4096, 2048, 1024, 512, 256, 128))
    tk = _pick(d, (512, 256, 128))
    cb = jnp.broadcast_to(jnp.stack([c[0], c_ctx])[:, :, None], (2, d, LANES))
    return pl.pallas_call(
        functools.partial(_mod_kernel, tn=tn),
        grid=(n // tn, d // tk),
        in_specs=[pl.BlockSpec((2, tk, LANES), lambda j, k: (0, k, 0)),
                  pl.BlockSpec((tk, tn), lambda j, k: (k, j)),
                  pl.BlockSpec((1, tn), lambda j, k: (0, j))],
        out_specs=pl.BlockSpec((SUBLANES, tn), lambda j, k: (0, j)),
        out_shape=jax.ShapeDtypeStruct((SUBLANES, n), F32),
        compiler_params=_params(2),
        name="modulation",
    )(cb, w_mod, b_mod.reshape(1, n))


def _premod_kernel(x_ref, sh_ref, sc_ref, o_ref, *, row):
    scale = 1.0 + sc_ref[row:row + 1, :]
    shift = sh_ref[row:row + 1, :]
    for r0 in range(0, x_ref.shape[0], LN_STRIP):
        rows = slice(r0, r0 + LN_STRIP)
        o_ref[rows, :] = (_ln_rows(x_ref[rows, :], ADALN_EPS) * scale + shift).astype(BF16)


def _premod(x2d, mods, row):
    m, d = x2d.shape
    tm = _pick(m, (256, 128))
    return pl.pallas_call(
        functools.partial(_premod_kernel, row=row),
        grid=(m // tm,),
        in_specs=[pl.BlockSpec((tm, d), lambda i: (i, 0)),
                  pl.BlockSpec((SUBLANES, d), lambda i: (0, 0)),
                  pl.BlockSpec((SUBLANES, d), lambda i: (0, 1))],
        out_specs=pl.BlockSpec((tm, d), lambda i: (i, 0)),
        out_shape=jax.ShapeDtypeStruct((m, d), BF16),
        compiler_params=_params(1),
        name="premod",
    )(x2d, mods, mods)


def _proj_kernel(*refs, rope, scale, tn):
    if rope:
        h_ref, w_ref, cos_ref, sa_ref, sb_ref, o_ref, wbf = refs
    else:
        h_ref, w_ref, o_ref, wbf = refs

    @pl.when(pl.program_id(1) == 0)
    def _():
        wbf[...] = w_ref[...].astype(BF16)

    if rope:
        half = h_ref.shape[0] // ROPE_ROW_PARTS
        for r0 in range(0, h_ref.shape[0], half):
            rows = slice(r0, r0 + half)
            acc = jnp.dot(h_ref[rows, :], wbf[...], preferred_element_type=F32)
            cos, sa, sb = cos_ref[rows, :], sa_ref[rows, :], sb_ref[rows, :]
            for c in range(tn // LANES):
                cols = slice(c * LANES, (c + 1) * LANES)
                a = acc[:, cols]
                r = a * cos + pltpu.roll(a, LANES - 16, 1) * sa + pltpu.roll(a, 16, 1) * sb
                o_ref[rows, cols] = (r * scale).astype(o_ref.dtype)
    else:
        o_ref[...] = jnp.dot(h_ref[...], wbf[...], preferred_element_type=F32).astype(o_ref.dtype)


def _project(h, w, col_off, n_cols, out_dtype, rope_tabs=None, scale=1.0):
    m, k = h.shape
    tn = _pick(math.gcd(n_cols, col_off), (512, 256, 128))
    tm = _pick(m, (1024, 512, 256, 128))
    off = col_off // tn
    in_specs = [pl.BlockSpec((tm, k), lambda j, i: (i, 0)),
                pl.BlockSpec((k, tn), lambda j, i: (0, j + off))]
    args = [h, w]
    if rope_tabs is not None:
        in_specs += [pl.BlockSpec((tm, LANES), lambda j, i: (i, 0))] * 3
        args += list(rope_tabs)
    return pl.pallas_call(
        functools.partial(_proj_kernel, rope=rope_tabs is not None, scale=scale, tn=tn),
        grid=(n_cols // tn, m // tm),
        in_specs=in_specs,
        out_specs=pl.BlockSpec((tm, tn), lambda j, i: (i, j)),
        out_shape=jax.ShapeDtypeStruct((m, n_cols), out_dtype),
        scratch_shapes=[pltpu.VMEM((k, tn), BF16)],
        compiler_params=_params(2),
        name="project",
    )(*args)


def _rope_tables(n_tokens):
    rows = n_tokens // GRID_W
    row = jnp.broadcast_to(jnp.arange(rows, dtype=F32)[:, None], (rows, GRID_W)).reshape(-1)
    col = jnp.broadcast_to(jnp.arange(GRID_W, dtype=F32)[None, :], (rows, GRID_W)).reshape(-1)
    axis_dim = HEAD_DIM // 2
    nf = axis_dim // 2
    inv_freq = ROPE_BASE ** (-jnp.arange(0, axis_dim, 2, dtype=F32) / axis_dim)
    ang = jnp.stack([row[:, None] * inv_freq, col[:, None] * inv_freq], axis=1)
    cos, sin = jnp.cos(ang), jnp.sin(ang)
    zero = jnp.zeros_like(sin)
    cos64 = jnp.concatenate([cos, cos], axis=-1).reshape(n_tokens, HEAD_DIM)
    sa64 = jnp.concatenate([-sin, zero], axis=-1).reshape(n_tokens, HEAD_DIM)
    sb64 = jnp.concatenate([zero, sin], axis=-1).reshape(n_tokens, HEAD_DIM)
    rep = LANES // HEAD_DIM
    return tuple(jnp.tile(t, (1, rep)) for t in (cos64, sa64, sb64))


def _attn_kernel(q_ref, k_ref, v_ref, kc_ref, vc_ref, dl_ref, g_ref, o_ref, qs_sc, vx_sc, sa_sc, sb_sc, sc_sc,
                 pa_sc, pb_sc, xa_sc, xb_sc, xc_sc, m_sc, acc_sc, *, tq, tk, n_chunks):
    s_len = n_chunks * tk
    c_len = kc_ref.shape[0]

    @pl.when(pl.program_id(1) == 0)
    def _():
        vx_sc[0:s_len, 0:V_DIM] = v_ref[...]
        vx_sc[s_len:s_len + c_len, 0:V_DIM] = vc_ref[...]
        vx_sc[:, V_DIM:2 * V_DIM] = jnp.ones((s_len + c_len, V_DIM), BF16)

    q = q_ref[...]
    lane = lax.broadcasted_iota(I32, q.shape, 1)
    zero = jnp.zeros_like(q)
    qs_sc[0:tq, :] = jnp.where(lane < HEAD_DIM, q, zero)
    qs_sc[tq:2 * tq, :] = jnp.where(lane >= HEAD_DIM, q, zero)
    m_sc[...] = jnp.full_like(m_sc, NEG_BIG)
    acc_sc[...] = jnp.zeros_like(acc_sc)

    def scores(s_ref, x_ref, kb):
        s = lax.dot_general(qs_sc[...], kb, (((1,), (1,)), ((), ())), preferred_element_type=F32)
        s_ref[...] = s
        x_ref[...] = jnp.broadcast_to(jnp.max(s, axis=-1, keepdims=True), x_ref.shape)

    def absorb(s_ref, x_ref, p_ref, v0):
        width = s_ref.shape[1]
        for r0 in range(0, 2 * tq, ATTN_STRIP):
            rows = slice(r0, r0 + ATTN_STRIP)
            m_prev = m_sc[rows, :]
            m_new = jnp.maximum(m_prev, x_ref[rows, :])
            x_ref[rows, :] = jnp.exp2(m_prev - m_new)
            m_sc[rows, :] = m_new
            p_ref[rows, 0:width] = jnp.exp2((s_ref[rows, :] - jnp.tile(m_new, (1, width // LANES))).astype(BF16))
        pv = jnp.dot(p_ref[:, 0:width], vx_sc[pl.ds(v0, width), :], preferred_element_type=F32)
        acc_sc[...] = jnp.tile(x_ref[...], (1, 2)) * acc_sc[...] + pv

    def kchunk(c):
        return k_ref[pl.ds(pl.multiple_of(c * tk, tk), tk), :]

    scores(sa_sc, xa_sc, kchunk(0))

    def body(j, carry):
        c = 2 * j
        scores(sb_sc, xb_sc, kchunk(c + 1))
        absorb(sa_sc, xa_sc, pa_sc, pl.multiple_of(c * tk, tk))
        scores(sa_sc, xa_sc, kchunk(c + 2))
        absorb(sb_sc, xb_sc, pb_sc, pl.multiple_of((c + 1) * tk, tk))
        return carry

    lax.fori_loop(0, (n_chunks - 2) // 2, body, 0)
    scores(sb_sc, xb_sc, kchunk(n_chunks - 1))
    absorb(sa_sc, xa_sc, pa_sc, (n_chunks - 2) * tk)
    scores(sc_sc, xc_sc, kc_ref[...])
    absorb(sb_sc, xb_sc, pb_sc, (n_chunks - 1) * tk)
    absorb(sc_sc, xc_sc, pa_sc, s_len)

    dl = dl_ref[...]
    lam = (jnp.exp(jnp.sum(dl[0:1] * dl[1:2], axis=-1, keepdims=True))
           - jnp.exp(jnp.sum(dl[2:3] * dl[3:4], axis=-1, keepdims=True)) + LAMBDA_INIT)
    o = (acc_sc[0:tq, 0:V_DIM] / acc_sc[0:tq, V_DIM:2 * V_DIM]
         - lam * (acc_sc[tq:2 * tq, 0:V_DIM] / acc_sc[tq:2 * tq, V_DIM:2 * V_DIM]))
    o = o * lax.rsqrt(jnp.mean(o * o, axis=-1, keepdims=True) + SUBLN_EPS)
    o_ref[...] = (o * g_ref[...] * (1.0 - LAMBDA_INIT)).astype(o_ref.dtype)


def _attention(q, k, v, kc, vc, diff_lambda, subln_g):
    s_len, width = q.shape
    c_len = kc.shape[0]
    tq = _pick(s_len, (512, 256, 128))
    tk = _pick(s_len // 2, (2048, 1024, 512, 256, 128))
    n_chunks = s_len // tk
    assert n_chunks % 2 == 0
    n_heads = width // V_DIM
    return pl.pallas_call(
        functools.partial(_attn_kernel, tq=tq, tk=tk, n_chunks=n_chunks),
        grid=(n_heads, s_len // tq),
        in_specs=[pl.BlockSpec((tq, V_DIM), lambda h, i: (i, h)),
                  pl.BlockSpec((s_len, V_DIM), lambda h, i: (0, h)),
                  pl.BlockSpec((s_len, V_DIM), lambda h, i: (0, h)),
                  pl.BlockSpec((c_len, V_DIM), lambda h, i: (0, h)),
                  pl.BlockSpec((c_len, V_DIM), lambda h, i: (0, h)),
                  pl.BlockSpec((4, HEAD_DIM), lambda h, i: (0, 0)),
                  pl.BlockSpec((1, V_DIM), lambda h, i: (0, 0))],
        out_specs=pl.BlockSpec((tq, V_DIM), lambda h, i: (i, h)),
        out_shape=jax.ShapeDtypeStruct((s_len, width), BF16),
        scratch_shapes=[pltpu.VMEM((2 * tq, V_DIM), BF16), pltpu.VMEM((s_len + c_len, 2 * V_DIM), BF16),
                        pltpu.VMEM((2 * tq, tk), F32), pltpu.VMEM((2 * tq, tk), F32),
                        pltpu.VMEM((2 * tq, c_len), F32)]
                       + [pltpu.VMEM((2 * tq, tk), BF16)] * 2
                       + [pltpu.VMEM((2 * tq, LANES), F32)] * 4
                       + [pltpu.VMEM((2 * tq, 2 * V_DIM), F32)],
        compiler_params=_params(2),
        name="diff_attention",
    )(q, k, v, kc, vc, diff_lambda, subln_g.reshape(1, V_DIM))


def _pool_kernel(p_ref, prev_ref, next_ref, w_ref, sc_ref, o_ref, ext, *, tm, n_tokens, gdim):
    i = pl.program_id(0)
    last = pl.num_programs(0) - 1
    ext[0:POOL_HALO, :] = jnp.where(i > 0, prev_ref[...], 0.0)
    ext[POOL_HALO:POOL_HALO + tm, :] = p_ref[...]
    ext[POOL_HALO + tm:2 * POOL_HALO + tm, :] = jnp.where(i < last, next_ref[...], 0.0)
    t = i * tm + lax.broadcasted_iota(I32, (tm, 1), 0)
    for g, w in enumerate(POOL_WINDOWS):
        cols = slice(g * gdim, (g + 1) * gdim)
        ssum = None
        for d in range(-(w // 2), w // 2):
            part = ext[POOL_HALO + d:POOL_HALO + d + tm, cols]
            ssum = part if ssum is None else ssum + part
        cnt = jnp.minimum(t + w // 2, n_tokens) - jnp.maximum(t - w // 2, 0)
        mean = ssum / cnt.astype(F32)
        mg = (mean - p_ref[:, cols]).astype(BF16)
        y = jnp.dot(mg, w_ref[g].astype(BF16), preferred_element_type=F32)
        o_ref[:, cols] = (y * sc_ref[:, cols]).astype(o_ref.dtype)


def _pool(p, w_pool, pool_scale):
    s_len, width = p.shape
    n_groups, gdim, _ = w_pool.shape
    assert n_groups == len(POOL_WINDOWS) and max(POOL_WINDOWS) // 2 <= POOL_HALO
    tm = _pick(s_len, (512, 256, 128))
    hb = tm // POOL_HALO
    n_halo_blocks = s_len // POOL_HALO
    return pl.pallas_call(
        functools.partial(_pool_kernel, tm=tm, n_tokens=s_len, gdim=gdim),
        grid=(s_len // tm,),
        in_specs=[pl.BlockSpec((tm, width), lambda i: (i, 0)),
                  pl.BlockSpec((POOL_HALO, width), lambda i: (jnp.maximum(i * hb - 1, 0), 0)),
                  pl.BlockSpec((POOL_HALO, width), lambda i: (jnp.minimum((i + 1) * hb, n_halo_blocks - 1), 0)),
                  pl.BlockSpec((n_groups, gdim, gdim), lambda i: (0, 0, 0)),
                  pl.BlockSpec((1, width), lambda i: (0, 0))],
        out_specs=pl.BlockSpec((tm, width), lambda i: (i, 0)),
        out_shape=jax.ShapeDtypeStruct((s_len, width), BF16),
        scratch_shapes=[pltpu.VMEM((tm + 2 * POOL_HALO, width), F32)],
        compiler_params=_params(1),
        name="pool_mixer",
    )(p, p, p, w_pool, pool_scale.reshape(1, width))


def _merge_kernel(a_ref, p_ref, wa_ref, wb_ref, ga_ref, gb_ref, o_ref, wabf, wbbf):
    @pl.when(pl.program_id(1) == 0)
    def _():
        wabf[...] = wa_ref[...].astype(BF16)
        wbbf[...] = wb_ref[...].astype(BF16)

    part = a_ref.shape[0] // MERGE_ROW_PARTS
    for r0 in range(0, a_ref.shape[0], part):
        rows = slice(r0, r0 + part)
        ya = jnp.dot(a_ref[rows, :], wabf[...], preferred_element_type=F32)
        yb = jnp.dot(p_ref[rows, :], wbbf[...], preferred_element_type=F32)
        o_ref[rows, :] = (jax.nn.sigmoid(ga_ref[rows, :]) * ya
                          + jax.nn.sigmoid(gb_ref[rows, :]) * yb).astype(o_ref.dtype)


def _merge(attn, pooled, w_a, w_b, gates):
    m, ka = attn.shape
    kb = pooled.shape[1]
    n = w_a.shape[1]
    tn = _pick(n, (512, 256, 128))
    tm = _pick(m, (1024, 512, 256, 128))
    gb_off = n // tn
    return pl.pallas_call(
        _merge_kernel,
        grid=(n // tn, m // tm),
        in_specs=[pl.BlockSpec((tm, ka), lambda j, i: (i, 0)),
                  pl.BlockSpec((tm, kb), lambda j, i: (i, 0)),
                  pl.BlockSpec((ka, tn), lambda j, i: (0, j)),
                  pl.BlockSpec((kb, tn), lambda j, i: (0, j)),
                  pl.BlockSpec((tm, tn), lambda j, i: (i, j)),
                  pl.BlockSpec((tm, tn), lambda j, i: (i, j + gb_off))],
        out_specs=pl.BlockSpec((tm, tn), lambda j, i: (i, j)),
        out_shape=jax.ShapeDtypeStruct((m, n), BF16),
        scratch_shapes=[pltpu.VMEM((ka, tn), BF16), pltpu.VMEM((kb, tn), BF16)],
        compiler_params=_params(2),
        name="gated_merge",
    )(attn, pooled, w_a, w_b, gates, gates)


def _post1_kernel(x_ref, y_ref, g1_ref, sh_ref, sc_ref, lg_ref, lb_ref, wr_ref, x1_ref, hp_ref, aff_ref, *, half):
    z = DEEPNORM_ALPHA * x_ref[...] + g1_ref[0:1, :] * y_ref[...]
    x1 = _ln_rows(z, LN_EPS) * lg_ref[...] + lb_ref[...]
    x1_ref[...] = x1
    h2 = _ln_rows(x1, ADALN_EPS) * (1.0 + sc_ref[0:1, :]) + sh_ref[0:1, :]
    h_hi = h2.astype(BF16)
    hb = lax.bitcast_convert_type(h_hi.astype(F32), U32)
    hp_ref[...] = (hb[:, :half] >> 16) | (hb[:, half:] & jnp.uint32(0xFFFF0000))
    h_lo = (h2 - h_hi.astype(F32)).astype(BF16)
    w = wr_ref[...]
    w_hi = w.astype(BF16)
    w_lo = (w - w_hi.astype(F32)).astype(BF16)
    nt = (((1,), (1,)), ((), ()))
    logits = (lax.dot_general(w_hi, h_hi, nt, preferred_element_type=F32)
              + lax.dot_general(w_lo, h_hi, nt, preferred_element_type=F32)
              + lax.dot_general(w_hi, h_lo, nt, preferred_element_type=F32))
    mx = jnp.max(logits, axis=0, keepdims=True)
    ex = jnp.exp(logits - mx)
    aff_ref[...] = ex / jnp.sum(ex, axis=0, keepdims=True)


def _post1(x2d, y, mods, ln_g, ln_b, w_router_t):
    m, d = x2d.shape
    n_exp = w_router_t.shape[0]
    tm = _pick(m, (256, 128))
    row = lambda i: (i, 0)
    vec = lambda i: (0, 0)
    return pl.pallas_call(
        functools.partial(_post1_kernel, half=d // 2),
        grid=(m // tm,),
        in_specs=[pl.BlockSpec((tm, d), row), pl.BlockSpec((tm, d), row),
                  pl.BlockSpec((SUBLANES, d), lambda i: (0, 2)),
                  pl.BlockSpec((SUBLANES, d), lambda i: (0, 3)),
                  pl.BlockSpec((SUBLANES, d), lambda i: (0, 4)),
                  pl.BlockSpec((1, d), vec), pl.BlockSpec((1, d), vec),
                  pl.BlockSpec((n_exp, d), vec)],
        out_specs=[pl.BlockSpec((tm, d), row), pl.BlockSpec((tm, d // 2), row),
                   pl.BlockSpec((n_exp, tm), lambda i: (0, i))],
        out_shape=[jax.ShapeDtypeStruct((m, d), F32), jax.ShapeDtypeStruct((m, d // 2), U32),
                   jax.ShapeDtypeStruct((n_exp, m), F32)],
        compiler_params=_params(1),
        name="ln1_router",
    )(x2d, y, mods, mods, mods, ln_g.reshape(1, d), ln_b.reshape(1, d), w_router_t)


def _route_kernel(aff_ref, idx_ref, csel_sc, *, n_exp, nc, cap):
    rows = n_exp * nc
    aff3 = aff_ref[...]
    ri = lax.broadcasted_iota(I32, (rows, rows), 0)
    rj = lax.broadcasted_iota(I32, (rows, rows), 1)
    before = jnp.where(((ri // nc) == (rj // nc)) & (rj < ri), 1.0, 0.0).astype(BF16)
    li = lax.broadcasted_iota(I32, (LANES, LANES), 0)
    lj = lax.broadcasted_iota(I32, (LANES, LANES), 1)
    tri = jnp.where(li <= lj, 1.0, 0.0).astype(BF16)
    ones = jnp.ones((LANES, LANES), BF16)

    def per_expert_sum(x3):
        return jnp.sum(jnp.sum(x3, axis=1, keepdims=True), axis=2, keepdims=True)

    def count(mask3):
        return per_expert_sum(jnp.where(mask3, 1.0, 0.0))

    def prefix(mask2):
        mb = jnp.where(mask2, 1.0, 0.0).astype(BF16)
        within = jnp.dot(mb, tri, preferred_element_type=F32)
        row_tot = jnp.dot(mb, ones, preferred_element_type=F32).astype(BF16)
        return within + jnp.dot(before, row_tot, preferred_element_type=F32)

    def rows_of(x3):
        return jnp.broadcast_to(x3, (n_exp, nc, LANES)).reshape(rows, LANES)

    top = jnp.max(jnp.max(aff3, axis=1, keepdims=True), axis=2, keepdims=True)
    lo0 = jnp.zeros_like(top)
    hi0 = 2.0 * top + 1e-30

    def bisect(_, carry):
        lo, hi = carry
        mid = 0.5 * (lo + hi)
        ok = count(aff3 >= mid) >= cap
        return jnp.where(ok, mid, lo), jnp.where(ok, hi, mid)

    thr3, _ = lax.fori_loop(0, BISECT_STEPS, bisect, (lo0, hi0))
    need = rows_of(cap - count(aff3 > thr3))
    aff2 = aff3.reshape(rows, LANES)
    thr2 = rows_of(thr3)
    gt = aff2 > thr2
    eq = aff2 == thr2
    sel = gt | (eq & (prefix(eq) <= need))
    csel_sc[...] = prefix(sel)

    lane = lax.broadcasted_iota(I32, (1, LANES), 1)
    lane_col = lax.broadcasted_iota(I32, (cap, LANES), 1)
    slot = lax.broadcasted_iota(I32, (cap, LANES), 0).astype(F32) + 1.0
    pad = jnp.zeros((LANES - nc, LANES), F32)

    def per_expert(e, carry):
        counts = jnp.concatenate([csel_sc[pl.ds(pl.multiple_of(e * nc, SUBLANES), nc), :], pad], axis=0)
        ends = jnp.where(lane < nc, counts.T[LANES - 1:LANES, :], float(2 * cap))
        chunk = jnp.sum(jnp.where(ends < slot, 1.0, 0.0), axis=-1, keepdims=True)
        pick = jnp.where(lane_col == chunk.astype(I32), 1.0, 0.0)
        row = jnp.dot(pick, counts, precision=lax.Precision.HIGHEST, preferred_element_type=F32)
        within = jnp.sum(jnp.where(row < slot, 1.0, 0.0), axis=-1, keepdims=True)
        idx_ref[e] = (chunk * float(LANES) + within).astype(I32)
        return carry

    lax.fori_loop(0, n_exp, per_expert, 0)


def _route(aff_t, cap):
    n_exp, s_len = aff_t.shape
    nc = s_len // LANES
    rows = n_exp * nc
    assert nc <= LANES and nc % SUBLANES == 0
    return pl.pallas_call(
        functools.partial(_route_kernel, n_exp=n_exp, nc=nc, cap=cap),
        grid=(1,),
        in_specs=[pl.BlockSpec((n_exp, nc, LANES), lambda i: (0, 0, 0))],
        out_specs=pl.BlockSpec((n_exp, cap, 1), lambda i: (0, 0, 0)),
        out_shape=jax.ShapeDtypeStruct((n_exp, cap, 1), I32),
        scratch_shapes=[pltpu.VMEM((rows, LANES), F32)],
        compiler_params=_params(1),
        name="expert_choice",
    )(aff_t.reshape(n_exp, nc, LANES))


def _ffn_kernel(idx_ref, hp_hbm, aff_hbm, wg_ref, wu_ref, wd_ref, o_ref, hsel, gsel, hb, gate_sc, act, sem,
                *, n_exp, cap, nf, nn, tf, half):
    e = pl.program_id(0)
    s = pl.program_id(1)
    n_steps = nf + nn
    rows_per_step = cap // nn
    nxt = jnp.minimum(e + 1, n_exp - 1)

    def start_row(ee, r):
        t = idx_ref[ee * cap + r]
        pltpu.make_async_copy(hp_hbm.at[pl.ds(t, 1), :], hsel.at[pl.ds(r, 1), :], sem.at[0]).start()
        pltpu.make_async_copy(aff_hbm.at[pl.ds(t, 1), :], gsel.at[pl.ds(r, 1), :], sem.at[1]).start()

    def wait_gather():
        pltpu.make_async_copy(hp_hbm.at[pl.ds(0, cap), :], hsel, sem.at[0]).wait()
        pltpu.make_async_copy(aff_hbm.at[pl.ds(0, cap), :], gsel, sem.at[1]).wait()

    def gather_slice():
        for r in range(rows_per_step):
            start_row(nxt, (s - nf) * rows_per_step + r)

    @pl.when((e == 0) & (s == 0))
    def _():
        def go(r, carry):
            start_row(0, r)
            return carry
        lax.fori_loop(0, cap, go, 0, unroll=GATHER_UNROLL)

    @pl.when(s == 0)
    def _():
        wait_gather()
        u = hsel[...]
        hb[:, :half] = lax.bitcast_convert_type(u << 16, F32).astype(BF16)
        hb[:, half:] = lax.bitcast_convert_type(u & jnp.uint32(0xFFFF0000), F32).astype(BF16)
        g = gsel[...]
        mine = jnp.where(lax.broadcasted_iota(I32, g.shape, 1) == e, g, 0.0)
        gate_sc[...] = jnp.broadcast_to(jnp.sum(mine, axis=-1, keepdims=True), g.shape)

    @pl.when(s < nf)
    def _():
        x = hb[...]
        u = jnp.dot(x, wg_ref[0].astype(BF16), preferred_element_type=F32)
        v = jnp.dot(x, wu_ref[0].astype(BF16), preferred_element_type=F32)
        act[s] = (u * jax.nn.sigmoid(u) * v).astype(BF16)

    @pl.when(s >= nf)
    def _():
        gather_slice()
        o = jnp.dot(act[0], wd_ref[0, 0:tf, :].astype(BF16), preferred_element_type=F32)
        for f in range(1, nf):
            o += jnp.dot(act[f], wd_ref[0, f * tf:(f + 1) * tf, :].astype(BF16), preferred_element_type=F32)
        o_ref[...] = (o * jnp.tile(gate_sc[...], (1, o.shape[1] // LANES))).astype(o_ref.dtype)

    @pl.when((e == n_exp - 1) & (s == n_steps - 1))
    def _():
        wait_gather()


def _expert_ffn(idx_flat, hp, aff_pad, w_gate, w_up, w_down):
    n_exp, d, ff = w_gate.shape
    cap = idx_flat.shape[0] // n_exp
    half = hp.shape[1]
    tf = _pick(ff, (256, 128))
    tn = _pick(d, (512, 256, 128))
    nf, nn = ff // tf, d // tn
    assert cap % nn == 0
    last_e = n_exp - 1

    def up_map(e, s, idx):
        return (jnp.where(s < nf, e, jnp.minimum(e + 1, last_e)), 0, jnp.where(s < nf, s, 0))

    def down_map(e, s, idx):
        return (e, 0, jnp.maximum(s - nf, 0))

    def out_map(e, s, idx):
        return (e, jnp.maximum(s - nf, 0))

    grid_spec = pltpu.PrefetchScalarGridSpec(
        num_scalar_prefetch=1,
        grid=(n_exp, nf + nn),
        in_specs=[pl.BlockSpec(memory_space=pl.ANY),
                  pl.BlockSpec(memory_space=pl.ANY),
                  pl.BlockSpec((1, d, tf), up_map),
                  pl.BlockSpec((1, d, tf), up_map),
                  pl.BlockSpec((1, ff, tn), down_map)],
        out_specs=pl.BlockSpec((cap, tn), out_map),
        scratch_shapes=[pltpu.VMEM((cap, half), U32), pltpu.VMEM((cap, LANES), F32),
                        pltpu.VMEM((cap, d), BF16), pltpu.VMEM((cap, LANES), F32),
                        pltpu.VMEM((nf, cap, tf), BF16), pltpu.SemaphoreType.DMA((2,))],
    )
    return pl.pallas_call(
        functools.partial(_ffn_kernel, n_exp=n_exp, cap=cap, nf=nf, nn=nn, tf=tf, half=half),
        grid_spec=grid_spec,
        out_shape=jax.ShapeDtypeStruct((n_exp * cap, d), BF16),
        compiler_params=_params(2),
        name="expert_ffn",
    )(idx_flat, hp, aff_pad, w_gate, w_up, w_down)


def _combine_kernel(off_ref, tot_ref, o_hbm, idx_hbm, x1_ref, g2_ref, lg_ref, lb_ref, out_ref, rbuf, ibuf, acc, sem,
                    *, tb, kbuf, max_pieces, n_blocks):
    b = pl.program_id(0)
    slot = b % 2
    pieces_per_round = kbuf // PIECE

    @pl.when(b == 0)
    def _():
        rbuf[...] = jnp.zeros_like(rbuf)
        ibuf[...] = jnp.full_like(ibuf, -1)

    def copies(blk, sl, j_global, j_local):
        src = pl.multiple_of(off_ref[blk * max_pieces + j_global], PIECE)
        dst = pl.multiple_of(j_local * PIECE, PIECE)
        return (pltpu.make_async_copy(o_hbm.at[pl.ds(src, PIECE), :], rbuf.at[sl, pl.ds(dst, PIECE), :],
                                      sem.at[0, sl]),
                pltpu.make_async_copy(idx_hbm.at[pl.ds(src, PIECE), :], ibuf.at[sl, pl.ds(dst, PIECE), :],
                                      sem.at[1, sl]))

    def pieces_in_round(blk, r):
        return jnp.clip(tot_ref[blk] - r * pieces_per_round, 0, pieces_per_round)

    def start_round(blk, sl, r):
        def go(j, c):
            for cp in copies(blk, sl, r * pieces_per_round + j, j):
                cp.start()
            return c
        lax.fori_loop(0, pieces_in_round(blk, r), go, 0)

    def wait_round(blk, sl, r):
        def go(j, c):
            for cp in copies(blk, sl, r * pieces_per_round + j, j):
                cp.wait()
            return c
        lax.fori_loop(0, pieces_in_round(blk, r), go, 0)

    @pl.when(b == 0)
    def _():
        start_round(0, 0, 0)

    @pl.when(b + 1 < n_blocks)
    def _():
        start_round(b + 1, 1 - slot, 0)

    acc[...] = jnp.zeros_like(acc)
    tok = b * tb + lax.broadcasted_iota(I32, (COMBINE_GROUP, tb), 1)

    def one_round(r, carry):
        @pl.when(r > 0)
        def _():
            start_round(b, slot, r)

        wait_round(b, slot, r)
        rows = pieces_in_round(b, r) * PIECE

        def group(g, c):
            r0 = pl.multiple_of(g * COMBINE_GROUP, COMBINE_GROUP)
            k = r0 + lax.broadcasted_iota(I32, (COMBINE_GROUP, tb), 0)
            ids = jnp.tile(ibuf[slot, pl.ds(r0, COMBINE_GROUP), :], (1, tb // LANES))
            onehot = jnp.where((ids == tok) & (k < rows), 1.0, 0.0).astype(BF16)
            acc[...] += lax.dot_general(onehot, rbuf[slot, pl.ds(r0, COMBINE_GROUP), :], (((0,), (0,)), ((), ())),
                                        preferred_element_type=F32)
            return c

        lax.fori_loop(0, (rows + COMBINE_GROUP - 1) // COMBINE_GROUP, group, 0)
        return carry

    lax.fori_loop(0, (tot_ref[b] + pieces_per_round - 1) // pieces_per_round, one_round, 0)
    z = DEEPNORM_ALPHA * x1_ref[...] + g2_ref[0:1, :] * acc[...]
    out_ref[...] = _ln_rows(z, LN_EPS) * lg_ref[...] + lb_ref[...]


def _combine(expert_out, idx, x1, mods, ln_g, ln_b):
    n_exp, cap = idx.shape
    s_len, d = x1.shape
    tb = _pick(s_len, (256, 128))
    nb = s_len // tb
    max_pieces = n_exp * (cap // PIECE)
    kbuf = min(COMBINE_ROWS, max_pieces * PIECE)

    edges = jnp.arange(nb + 1, dtype=I32) * tb
    below = jnp.sum((idx[None, :, :] < edges[:, None, None]).astype(I32), axis=2)
    s0, s1 = below[:-1], below[1:]
    p0 = s0 // PIECE
    n_pc = jnp.where(s1 > s0, (s1 + PIECE - 1) // PIECE - p0, 0)
    cum = jnp.cumsum(n_pc, axis=1)
    total = cum[:, -1]
    j = jnp.arange(max_pieces, dtype=I32)[None, :]
    e_of = jnp.minimum(jnp.sum((j[:, :, None] >= cum[:, None, :]).astype(I32), axis=2), n_exp - 1)
    mine = e_of[:, :, None] == jnp.arange(n_exp, dtype=I32)[None, None, :]
    first_piece = jnp.sum(jnp.where(mine, (p0 - (cum - n_pc))[:, None, :], 0), axis=2)
    off = jnp.where(j < total[:, None], e_of * cap + (first_piece + j) * PIECE, 0).astype(I32)

    idx_rep = jnp.broadcast_to(idx.reshape(n_exp * cap, 1), (n_exp * cap, LANES))
    grid_spec = pltpu.PrefetchScalarGridSpec(
        num_scalar_prefetch=2,
        grid=(nb,),
        in_specs=[pl.BlockSpec(memory_space=pl.ANY),
                  pl.BlockSpec(memory_space=pl.ANY),
                  pl.BlockSpec((tb, d), lambda i, off, tot: (i, 0)),
                  pl.BlockSpec((SUBLANES, d), lambda i, off, tot: (0, 5)),
                  pl.BlockSpec((1, d), lambda i, off, tot: (0, 0)),
                  pl.BlockSpec((1, d), lambda i, off, tot: (0, 0))],
        out_specs=pl.BlockSpec((tb, d), lambda i, off, tot: (i, 0)),
        scratch_shapes=[pltpu.VMEM((2, kbuf, d), BF16), pltpu.VMEM((2, kbuf, LANES), I32),
                        pltpu.VMEM((tb, d), F32), pltpu.SemaphoreType.DMA((2, 2))],
    )
    return pl.pallas_call(
        functools.partial(_combine_kernel, tb=tb, kbuf=kbuf, max_pieces=max_pieces, n_blocks=nb),
        grid_spec=grid_spec,
        out_shape=jax.ShapeDtypeStruct((s_len, d), F32),
        compiler_params=_params(1),
        name="combine_ln2",
    )(off.reshape(-1), total.astype(I32), expert_out, idx_rep, x1, mods, ln_g.reshape(1, d), ln_b.reshape(1, d))


def kernel(x, c, ctx, c_ctx, w_mod, b_mod, w_in, diff_lambda, diff_subln_g, w_pool, pool_scale, w_branch_attn,
           w_branch_pool, w_out, ln1_g, ln1_b, w_router, w_gate, w_up, w_down, ln2_g, ln2_b):
    batch, s_len, d = x.shape
    assert batch == 1 and w_mod.shape[0] == DEPTH
    attn_w = N_HEADS * V_DIM
    pool_w = w_pool.shape[1] * w_pool.shape[2]
    k_off, v_off, p_off = attn_w, 2 * attn_w, 3 * attn_w
    g_off = p_off + pool_w
    n_exp = w_router.shape[-1]
    cap = EC_CAPACITY_FACTOR * s_len // n_exp

    mods = _modulation(c, c_ctx, w_mod[0], b_mod[0])
    h = _premod(x[0], mods, 0)
    hc = _premod(ctx[0], mods, 1)

    w = w_in[0]
    tabs = _rope_tables(s_len)
    q = _project(h, w, 0, attn_w, BF16, tabs, HEAD_DIM ** -0.5 * math.log2(math.e))
    k = _project(h, w, k_off, attn_w, BF16, tabs)
    v = _project(h, w, v_off, attn_w, BF16)
    kc = _project(hc, w, k_off, attn_w, BF16)
    vc = _project(hc, w, v_off, attn_w, BF16)
    p = _project(h, w, p_off, pool_w, F32)
    gates = _project(h, w, g_off, 2 * d, F32)

    attn = _attention(q, k, v, kc, vc, diff_lambda[0], diff_subln_g[0])
    pooled = _pool(p, w_pool[0], pool_scale[0])
    merged = _merge(attn, pooled, w_branch_attn[0], w_branch_pool[0], gates)
    y = _project(merged, w_out[0], 0, d, F32)

    x1, hp, aff_t = _post1(x[0], y, mods, ln1_g[0], ln1_b[0], w_router[0].T)
    idx = _route(aff_t, cap).reshape(n_exp, cap)
    aff_pad = jnp.pad(aff_t.T, ((0, 0), (0, LANES - n_exp)))
    expert_out = _expert_ffn(idx.reshape(-1), hp, aff_pad, w_gate[0], w_up[0], w_down[0])
    out = _combine(expert_out, idx, x1, mods, ln2_g[0], ln2_b[0])
    return out.reshape(batch, s_len, d)
```

```python
import functools
import math

import jax
import jax.numpy as jnp
from jax import lax
from jax.experimental import pallas as pl
from jax.experimental.pallas import tpu as pltpu

F32 = jnp.float32
BF16 = jnp.bfloat16
I32 = jnp.int32
U32 = jnp.uint32

LANES = 128
SUBLANES = 8
VMEM_LIMIT = 56 * 1024 * 1024

GRID_W = 64
N_HEADS = 16
HEAD_DIM = 64
V_DIM = 2 * HEAD_DIM
POOL_WINDOWS = (2, 4, 8, 16)
POOL_HALO = 8
EC_CAPACITY_FACTOR = 2
ROPE_BASE = 10000.0
LN_EPS = 1e-5
ADALN_EPS = 1e-6
SUBLN_EPS = 1e-5
DEPTH = 1
DEEPNORM_ALPHA = (2.0 * DEPTH) ** 0.25
LAMBDA_INIT = 0.8 - 0.6 * math.exp(-0.3 * 0)
NEG_BIG = -1e30
BISECT_STEPS = 64
GATHER_UNROLL = 8
ROPE_ROW_PARTS = 8
MERGE_ROW_PARTS = 4
PIECE = 16
COMBINE_GROUP = 256
COMBINE_ROWS = 1024
LN_STRIP = 16
ATTN_STRIP = 32


def _pick(n, prefs):
    for p in prefs:
        if n % p == 0:
            return p
    raise ValueError(f"no tile for {n} in {prefs}")


def _params(n_axes):
    return pltpu.CompilerParams(dimension_semantics=("arbitrary",) * n_axes,
                                vmem_limit_bytes=VMEM_LIMIT)


def _ln_rows(x, eps):
    mu = jnp.mean(x, axis=-1, keepdims=True)
    xc = x - mu
    var = jnp.mean(xc * xc, axis=-1, keepdims=True)
    return xc * lax.rsqrt(var + eps)


def _mod_kernel(cb_ref, w_ref, b_ref, o_ref, *, tn):
    @pl.when(pl.program_id(1) == 0)
    def _():
        o_ref[...] = jnp.broadcast_to(b_ref[...], o_ref.shape)

    w = w_ref[...]
    for i in range(2):
        cv = cb_ref[i]
        s = jnp.tile(cv * jax.nn.sigmoid(cv), (1, tn // LANES))
        o_ref[i:i + 1, :] += jnp.sum(w * s, axis=0, keepdims=True)


def _modulation(c, c_ctx, w_mod, b_mod):
    d, n = w_mod.shape
    tn = _pick(n, (4096, 2048, 1024, 512, 256, 128))
    tk = _pick(d, (512, 256, 128))
    cb = jnp.broadcast_to(jnp.stack([c[0], c_ctx])[:, :, None], (2, d, LANES))
    return pl.pallas_call(
        functools.partial(_mod_kernel, tn=tn),
        grid=(n // tn, d // tk),
        in_specs=[pl.BlockSpec((2, tk, LANES), lambda j, k: (0, k, 0)),
                  pl.BlockSpec((tk, tn), lambda j, k: (k, j)),
                  pl.BlockSpec((1, tn), lambda j, k: (0, j))],
        out_specs=pl.BlockSpec((SUBLANES, tn), lambda j, k: (0, j)),
        out_shape=jax.ShapeDtypeStruct((SUBLANES, n), F32),
        compiler_params=_params(2),
        name="modulation",
    )(cb, w_mod, b_mod.reshape(1, n))


def _premod_kernel(x_ref, sh_ref, sc_ref, o_ref, *, row):
    scale = 1.0 + sc_ref[row:row + 1, :]
    shift = sh_ref[row:row + 1, :]
    for r0 in range(0, x_ref.shape[0], LN_STRIP):
        rows = slice(r0, r0 + LN_STRIP)
        o_ref[rows, :] = (_ln_rows(x_ref[rows, :], ADALN_EPS) * scale + shift).astype(BF16)


def _premod(x2d, mods, row):
    m, d = x2d.shape
    tm = _pick(m, (256, 128))
    return pl.pallas_call(
        functools.partial(_premod_kernel, row=row),
        grid=(m // tm,),
        in_specs=[pl.BlockSpec((tm, d), lambda i: (i, 0)),
                  pl.BlockSpec((SUBLANES, d), lambda i: (0, 0)),
                  pl.BlockSpec((SUBLANES, d), lambda i: (0, 1))],
        out_specs=pl.BlockSpec((tm, d), lambda i: (i, 0)),
        out_shape=jax.ShapeDtypeStruct((m, d), BF16),
        compiler_params=_params(1),
        name="premod",
    )(x2d, mods, mods)


def _proj_kernel(*refs, rope, scale, tn):
    if rope:
        h_ref, w_ref, cos_ref, sa_ref, sb_ref, o_ref, wbf = refs
    else:
        h_ref, w_ref, o_ref, wbf = refs

    @pl.when(pl.program_id(1) == 0)
    def _():
        wbf[...] = w_ref[...].astype(BF16)

    if rope:
        half = h_ref.shape[0] // ROPE_ROW_PARTS
        for r0 in range(0, h_ref.shape[0], half):
            rows = slice(r0, r0 + half)
            acc = jnp.dot(h_ref[rows, :], wbf[...], preferred_element_type=F32)
            cos, sa, sb = cos_ref[rows, :], sa_ref[rows, :], sb_ref[rows, :]
            for c in range(tn // LANES):
                cols = slice(c * LANES, (c + 1) * LANES)
                a = acc[:, cols]
                r = a * cos + pltpu.roll(a, LANES - 16, 1) * sa + pltpu.roll(a, 16, 1) * sb
                o_ref[rows, cols] = (r * scale).astype(o_ref.dtype)
    else:
        o_ref[...] = jnp.dot(h_ref[...], wbf[...], preferred_element_type=F32).astype(o_ref.dtype)


def _project(h, w, col_off, n_cols, out_dtype, rope_tabs=None, scale=1.0):
    m, k = h.shape
    tn = _pick(math.gcd(n_cols, col_off), (512, 256, 128))
    tm = _pick(m, (1024, 512, 256, 128))
    off = col_off // tn
    in_specs = [pl.BlockSpec((tm, k), lambda j, i: (i, 0)),
                pl.BlockSpec((k, tn), lambda j, i: (0, j + off))]
    args = [h, w]
    if rope_tabs is not None:
        in_specs += [pl.BlockSpec((tm, LANES), lambda j, i: (i, 0))] * 3
        args += list(rope_tabs)
    return pl.pallas_call(
        functools.partial(_proj_kernel, rope=rope_tabs is not None, scale=scale, tn=tn),
        grid=(n_cols // tn, m // tm),
        in_specs=in_specs,
        out_specs=pl.BlockSpec((tm, tn), lambda j, i: (i, j)),
        out_shape=jax.ShapeDtypeStruct((m, n_cols), out_dtype),
        scratch_shapes=[pltpu.VMEM((k, tn), BF16)],
        compiler_params=_params(2),
        name="project",
    )(*args)


def _rope_tables(n_tokens):
    rows = n_tokens // GRID_W
    row = jnp.broadcast_to(jnp.arange(rows, dtype=F32)[:, None], (rows, GRID_W)).reshape(-1)
    col = jnp.broadcast_to(jnp.arange(GRID_W, dtype=F32)[None, :], (rows, GRID_W)).reshape(-1)
    axis_dim = HEAD_DIM // 2
    nf = axis_dim // 2
    inv_freq = ROPE_BASE ** (-jnp.arange(0, axis_dim, 2, dtype=F32) / axis_dim)
    ang = jnp.stack([row[:, None] * inv_freq, col[:, None] * inv_freq], axis=1)
    cos, sin = jnp.cos(ang), jnp.sin(ang)
    zero = jnp.zeros_like(sin)
    cos64 = jnp.concatenate([cos, cos], axis=-1).reshape(n_tokens, HEAD_DIM)
    sa64 = jnp.concatenate([-sin, zero], axis=-1).reshape(n_tokens, HEAD_DIM)
    sb64 = jnp.concatenate([zero, sin], axis=-1).reshape(n_tokens, HEAD_DIM)
    rep = LANES // HEAD_DIM
    return tuple(jnp.tile(t, (1, rep)) for t in (cos64, sa64, sb64))


def _attn_kernel(q_ref, k_ref, v_ref, kc_ref, vc_ref, dl_ref, g_ref, o_ref, qs_sc, vx_sc, sa_sc, sb_sc, sc_sc,
                 pa_sc, pb_sc, xa_sc, xb_sc, xc_sc, m_sc, acc_sc, *, tq, tk, n_chunks):
    s_len = n_chunks * tk
    c_len = kc_ref.shape[0]

    @pl.when(pl.program_id(1) == 0)
    def _():
        vx_sc[0:s_len, 0:V_DIM] = v_ref[...]
        vx_sc[s_len:s_len + c_len, 0:V_DIM] = vc_ref[...]
        vx_sc[:, V_DIM:2 * V_DIM] = jnp.ones((s_len + c_len, V_DIM), BF16)

    q = q_ref[...]
    lane = lax.broadcasted_iota(I32, q.shape, 1)
    zero = jnp.zeros_like(q)
    qs_sc[0:tq, :] = jnp.where(lane < HEAD_DIM, q, zero)
    qs_sc[tq:2 * tq, :] = jnp.where(lane >= HEAD_DIM, q, zero)
    m_sc[...] = jnp.full_like(m_sc, NEG_BIG)
    acc_sc[...] = jnp.zeros_like(acc_sc)

    def scores(s_ref, x_ref, kb):
        s = lax.dot_general(qs_sc[...], kb, (((1,), (1,)), ((), ())), preferred_element_type=F32)
        s_ref[...] = s
        x_ref[...] = jnp.broadcast_to(jnp.max(s, axis=-1, keepdims=True), x_ref.shape)

    def absorb(s_ref, x_ref, p_ref, v0):
        width = s_ref.shape[1]
        for r0 in range(0, 2 * tq, ATTN_STRIP):
            rows = slice(r0, r0 + ATTN_STRIP)
            m_prev = m_sc[rows, :]
            m_new = jnp.maximum(m_prev, x_ref[rows, :])
            x_ref[rows, :] = jnp.exp2(m_prev - m_new)
            m_sc[rows, :] = m_new
            p_ref[rows, 0:width] = jnp.exp2((s_ref[rows, :] - jnp.tile(m_new, (1, width // LANES))).astype(BF16))
        pv = jnp.dot(p_ref[:, 0:width], vx_sc[pl.ds(v0, width), :], preferred_element_type=F32)
        acc_sc[...] = jnp.tile(x_ref[...], (1, 2)) * acc_sc[...] + pv

    def kchunk(c):
        return k_ref[pl.ds(pl.multiple_of(c * tk, tk), tk), :]

    scores(sa_sc, xa_sc, kchunk(0))

    def body(j, carry):
        c = 2 * j
        scores(sb_sc, xb_sc, kchunk(c + 1))
        absorb(sa_sc, xa_sc, pa_sc, pl.multiple_of(c * tk, tk))
        scores(sa_sc, xa_sc, kchunk(c + 2))
        absorb(sb_sc, xb_sc, pb_sc, pl.multiple_of((c + 1) * tk, tk))
        return carry

    lax.fori_loop(0, (n_chunks - 2) // 2, body, 0)
    scores(sb_sc, xb_sc, kchunk(n_chunks - 1))
    absorb(sa_sc, xa_sc, pa_sc, (n_chunks - 2) * tk)
    scores(sc_sc, xc_sc, kc_ref[...])
    absorb(sb_sc, xb_sc, pb_sc, (n_chunks - 1) * tk)
    absorb(sc_sc, xc_sc, pa_sc, s_len)

    dl = dl_ref[...]
    lam = (jnp.exp(jnp.sum(dl[0:1] * dl[1:2], axis=-1, keepdims=True))
           - jnp.exp(jnp.sum(dl[2:3] * dl[3:4], axis=-1, keepdims=True)) + LAMBDA_INIT)
    o = (acc_sc[0:tq, 0:V_DIM] / acc_sc[0:tq, V_DIM:2 * V_DIM]
         - lam * (acc_sc[tq:2 * tq, 0:V_DIM] / acc_sc[tq:2 * tq, V_DIM:2 * V_DIM]))
    o = o * lax.rsqrt(jnp.mean(o * o, axis=-1, keepdims=True) + SUBLN_EPS)
    o_ref[...] = (o * g_ref[...] * (1.0 - LAMBDA_INIT)).astype(o_ref.dtype)


def _attention(q, k, v, kc, vc, diff_lambda, subln_g):
    s_len, width = q.shape
    c_len = kc.shape[0]
    tq = _pick(s_len, (512, 256, 128))
    tk = _pick(s_len // 2, (2048, 1024, 512, 256, 128))
    n_chunks = s_len // tk
    assert n_chunks % 2 == 0
    n_heads = width // V_DIM
    return pl.pallas_call(
        functools.partial(_attn_kernel, tq=tq, tk=tk, n_chunks=n_chunks),
        grid=(n_heads, s_len // tq),
        in_specs=[pl.BlockSpec((tq, V_DIM), lambda h, i: (i, h)),
                  pl.BlockSpec((s_len, V_DIM), lambda h, i: (0, h)),
                  pl.BlockSpec((s_len, V_DIM), lambda h, i: (0, h)),
                  pl.BlockSpec((c_len, V_DIM), lambda h, i: (0, h)),
                  pl.BlockSpec((c_len, V_DIM), lambda h, i: (0, h)),
                  pl.BlockSpec((4, HEAD_DIM), lambda h, i: (0, 0)),
                  pl.BlockSpec((1, V_DIM), lambda h, i: (0, 0))],
        out_specs=pl.BlockSpec((tq, V_DIM), lambda h, i: (i, h)),
        out_shape=jax.ShapeDtypeStruct((s_len, width), BF16),
        scratch_shapes=[pltpu.VMEM((2 * tq, V_DIM), BF16), pltpu.VMEM((s_len + c_len, 2 * V_DIM), BF16),
                        pltpu.VMEM((2 * tq, tk), F32), pltpu.VMEM((2 * tq, tk), F32),
                        pltpu.VMEM((2 * tq, c_len), F32)]
                       + [pltpu.VMEM((2 * tq, tk), BF16)] * 2
                       + [pltpu.VMEM((2 * tq, LANES), F32)] * 4
                       + [pltpu.VMEM((2 * tq, 2 * V_DIM), F32)],
        compiler_params=_params(2),
        name="diff_attention",
    )(q, k, v, kc, vc, diff_lambda, subln_g.reshape(1, V_DIM))


def _pool_kernel(p_ref, prev_ref, next_ref, w_ref, sc_ref, o_ref, ext, *, tm, n_tokens, gdim):
    i = pl.program_id(0)
    last = pl.num_programs(0) - 1
    ext[0:POOL_HALO, :] = jnp.where(i > 0, prev_ref[...], 0.0)
    ext[POOL_HALO:POOL_HALO + tm, :] = p_ref[...]
    ext[POOL_HALO + tm:2 * POOL_HALO + tm, :] = jnp.where(i < last, next_ref[...], 0.0)
    t = i * tm + lax.broadcasted_iota(I32, (tm, 1), 0)
    for g, w in enumerate(POOL_WINDOWS):
        cols = slice(g * gdim, (g + 1) * gdim)
        ssum = None
        for d in range(-(w // 2), w // 2):
            part = ext[POOL_HALO + d:POOL_HALO + d + tm, cols]
            ssum = part if ssum is None else ssum + part
        cnt = jnp.minimum(t + w // 2, n_tokens) - jnp.maximum(t - w // 2, 0)
        mean = ssum / cnt.astype(F32)
        mg = (mean - p_ref[:, cols]).astype(BF16)
        y = jnp.dot(mg, w_ref[g].astype(BF16), preferred_element_type=F32)
        o_ref[:, cols] = (y * sc_ref[:, cols]).astype(o_ref.dtype)


def _pool(p, w_pool, pool_scale):
    s_len, width = p.shape
    n_groups, gdim, _ = w_pool.shape
    assert n_groups == len(POOL_WINDOWS) and max(POOL_WINDOWS) // 2 <= POOL_HALO
    tm = _pick(s_len, (512, 256, 128))
    hb = tm // POOL_HALO
    n_halo_blocks = s_len // POOL_HALO
    return pl.pallas_call(
        functools.partial(_pool_kernel, tm=tm, n_tokens=s_len, gdim=gdim),
        grid=(s_len // tm,),
        in_specs=[pl.BlockSpec((tm, width), lambda i: (i, 0)),
                  pl.BlockSpec((POOL_HALO, width), lambda i: (jnp.maximum(i * hb - 1, 0), 0)),
                  pl.BlockSpec((POOL_HALO, width), lambda i: (jnp.minimum((i + 1) * hb, n_halo_blocks - 1), 0)),
                  pl.BlockSpec((n_groups, gdim, gdim), lambda i: (0, 0, 0)),
                  pl.BlockSpec((1, width), lambda i: (0, 0))],
        out_specs=pl.BlockSpec((tm, width), lambda i: (i, 0)),
        out_shape=jax.ShapeDtypeStruct((s_len, width), BF16),
        scratch_shapes=[pltpu.VMEM((tm + 2 * POOL_HALO, width), F32)],
        compiler_params=_params(1),
        name="pool_mixer",
    )(p, p, p, w_pool, pool_scale.reshape(1, width))


def _merge_kernel(a_ref, p_ref, wa_ref, wb_ref, ga_ref, gb_ref, o_ref, wabf, wbbf):
    @pl.when(pl.program_id(1) == 0)
    def _():
        wabf[...] = wa_ref[...].astype(BF16)
        wbbf[...] = wb_ref[...].astype(BF16)

    part = a_ref.shape[0] // MERGE_ROW_PARTS
    for r0 in range(0, a_ref.shape[0], part):
        rows = slice(r0, r0 + part)
        ya = jnp.dot(a_ref[rows, :], wabf[...], preferred_element_type=F32)
        yb = jnp.dot(p_ref[rows, :], wbbf[...], preferred_element_type=F32)
        o_ref[rows, :] = (jax.nn.sigmoid(ga_ref[rows, :]) * ya
                          + jax.nn.sigmoid(gb_ref[rows, :]) * yb).astype(o_ref.dtype)


def _merge(attn, pooled, w_a, w_b, gates):
    m, ka = attn.shape
    kb = pooled.shape[1]
    n = w_a.shape[1]
    tn = _pick(n, (512, 256, 128))
    tm = _pick(m, (1024, 512, 256, 128))
    gb_off = n // tn
    return pl.pallas_call(
        _merge_kernel,
        grid=(n // tn, m // tm),
        in_specs=[pl.BlockSpec((tm, ka), lambda j, i: (i, 0)),
                  pl.BlockSpec((tm, kb), lambda j, i: (i, 0)),
                  pl.BlockSpec((ka, tn), lambda j, i: (0, j)),
                  pl.BlockSpec((kb, tn), lambda j, i: (0, j)),
                  pl.BlockSpec((tm, tn), lambda j, i: (i, j)),
                  pl.BlockSpec((tm, tn), lambda j, i: (i, j + gb_off))],
        out_specs=pl.BlockSpec((tm, tn), lambda j, i: (i, j)),
        out_shape=jax.ShapeDtypeStruct((m, n), BF16),
        scratch_shapes=[pltpu.VMEM((ka, tn), BF16), pltpu.VMEM((kb, tn), BF16)],
        compiler_params=_params(2),
        name="gated_merge",
    )(attn, pooled, w_a, w_b, gates, gates)


def _post1_kernel(x_ref, y_ref, g1_ref, sh_ref, sc_ref, lg_ref, lb_ref, wr_ref, x1_ref, hp_ref, aff_ref, *, half):
    z = DEEPNORM_ALPHA * x_ref[...] + g1_ref[0:1, :] * y_ref[...]
    x1 = _ln_rows(z, LN_EPS) * lg_ref[...] + lb_ref[...]
    x1_ref[...] = x1
    h2 = _ln_rows(x1, ADALN_EPS) * (1.0 + sc_ref[0:1, :]) + sh_ref[0:1, :]
    h_hi = h2.astype(BF16)
    hb = lax.bitcast_convert_type(h_hi.astype(F32), U32)
    hp_ref[...] = (hb[:, :half] >> 16) | (hb[:, half:] & jnp.uint32(0xFFFF0000))
    h_lo = (h2 - h_hi.astype(F32)).astype(BF16)
    w = wr_ref[...]
    w_hi = w.astype(BF16)
    w_lo = (w - w_hi.astype(F32)).astype(BF16)
    nt = (((1,), (1,)), ((), ()))
    logits = (lax.dot_general(w_hi, h_hi, nt, preferred_element_type=F32)
              + lax.dot_general(w_lo, h_hi, nt, preferred_element_type=F32)
              + lax.dot_general(w_hi, h_lo, nt, preferred_element_type=F32))
    mx = jnp.max(logits, axis=0, keepdims=True)
    ex = jnp.exp(logits - mx)
    aff_ref[...] = ex / jnp.sum(ex, axis=0, keepdims=True)


def _post1(x2d, y, mods, ln_g, ln_b, w_router_t):
    m, d = x2d.shape
    n_exp = w_router_t.shape[0]
    tm = _pick(m, (256, 128))
    row = lambda i: (i, 0)
    vec = lambda i: (0, 0)
    return pl.pallas_call(
        functools.partial(_post1_kernel, half=d // 2),
        grid=(m // tm,),
        in_specs=[pl.BlockSpec((tm, d), row), pl.BlockSpec((tm, d), row),
                  pl.BlockSpec((SUBLANES, d), lambda i: (0, 2)),
                  pl.BlockSpec((SUBLANES, d), lambda i: (0, 3)),
                  pl.BlockSpec((SUBLANES, d), lambda i: (0, 4)),
                  pl.BlockSpec((1, d), vec), pl.BlockSpec((1, d), vec),
                  pl.BlockSpec((n_exp, d), vec)],
        out_specs=[pl.BlockSpec((tm, d), row), pl.BlockSpec((tm, d // 2), row),
                   pl.BlockSpec((n_exp, tm), lambda i: (0, i))],
        out_shape=[jax.ShapeDtypeStruct((m, d), F32), jax.ShapeDtypeStruct((m, d // 2), U32),
                   jax.ShapeDtypeStruct((n_exp, m), F32)],
        compiler_params=_params(1),
        name="ln1_router",
    )(x2d, y, mods, mods, mods, ln_g.reshape(1, d), ln_b.reshape(1, d), w_router_t)


def _route_kernel(aff_ref, idx_ref, csel_sc, *, n_exp, nc, cap):
    rows = n_exp * nc
    aff3 = aff_ref[...]
    ri = lax.broadcasted_iota(I32, (rows, rows), 0)
    rj = lax.broadcasted_iota(I32, (rows, rows), 1)
    before = jnp.where(((ri // nc) == (rj // nc)) & (rj < ri), 1.0, 0.0).astype(BF16)
    li = lax.broadcasted_iota(I32, (LANES, LANES), 0)
    lj = lax.broadcasted_iota(I32, (LANES, LANES), 1)
    tri = jnp.where(li <= lj, 1.0, 0.0).astype(BF16)
    ones = jnp.ones((LANES, LANES), BF16)

    def per_expert_sum(x3):
        return jnp.sum(jnp.sum(x3, axis=1, keepdims=True), axis=2, keepdims=True)

    def count(mask3):
        return per_expert_sum(jnp.where(mask3, 1.0, 0.0))

    def prefix(mask2):
        mb = jnp.where(mask2, 1.0, 0.0).astype(BF16)
        within = jnp.dot(mb, tri, preferred_element_type=F32)
        row_tot = jnp.dot(mb, ones, preferred_element_type=F32).astype(BF16)
        return within + jnp.dot(before, row_tot, preferred_element_type=F32)

    def rows_of(x3):
        return jnp.broadcast_to(x3, (n_exp, nc, LANES)).reshape(rows, LANES)

    top = jnp.max(jnp.max(aff3, axis=1, keepdims=True), axis=2, keepdims=True)
    lo0 = jnp.zeros_like(top)
    hi0 = 2.0 * top + 1e-30

    def bisect(_, carry):
        lo, hi = carry
        mid = 0.5 * (lo + hi)
        ok = count(aff3 >= mid) >= cap
        return jnp.where(ok, mid, lo), jnp.where(ok, hi, mid)

    thr3, _ = lax.fori_loop(0, BISECT_STEPS, bisect, (lo0, hi0))
    need = rows_of(cap - count(aff3 > thr3))
    aff2 = aff3.reshape(rows, LANES)
    thr2 = rows_of(thr3)
    gt = aff2 > thr2
    eq = aff2 == thr2
    sel = gt | (eq & (prefix(eq) <= need))
    csel_sc[...] = prefix(sel)

    lane = lax.broadcasted_iota(I32, (1, LANES), 1)
    lane_col = lax.broadcasted_iota(I32, (cap, LANES), 1)
    slot = lax.broadcasted_iota(I32, (cap, LANES), 0).astype(F32) + 1.0
    pad = jnp.zeros((LANES - nc, LANES), F32)

    def per_expert(e, carry):
        counts = jnp.concatenate([csel_sc[pl.ds(pl.multiple_of(e * nc, SUBLANES), nc), :], pad], axis=0)
        ends = jnp.where(lane < nc, counts.T[LANES - 1:LANES, :], float(2 * cap))
        chunk = jnp.sum(jnp.where(ends < slot, 1.0, 0.0), axis=-1, keepdims=True)
        pick = jnp.where(lane_col == chunk.astype(I32), 1.0, 0.0)
        row = jnp.dot(pick, counts, precision=lax.Precision.HIGHEST, preferred_element_type=F32)
        within = jnp.sum(jnp.where(row < slot, 1.0, 0.0), axis=-1, keepdims=True)
        idx_ref[e] = (chunk * float(LANES) + within).astype(I32)
        return carry

    lax.fori_loop(0, n_exp, per_expert, 0)


def _route(aff_t, cap):
    n_exp, s_len = aff_t.shape
    nc = s_len // LANES
    rows = n_exp * nc
    assert nc <= LANES and nc % SUBLANES == 0
    return pl.pallas_call(
        functools.partial(_route_kernel, n_exp=n_exp, nc=nc, cap=cap),
        grid=(1,),
        in_specs=[pl.BlockSpec((n_exp, nc, LANES), lambda i: (0, 0, 0))],
        out_specs=pl.BlockSpec((n_exp, cap, 1), lambda i: (0, 0, 0)),
        out_shape=jax.ShapeDtypeStruct((n_exp, cap, 1), I32),
        scratch_shapes=[pltpu.VMEM((rows, LANES), F32)],
        compiler_params=_params(1),
        name="expert_choice",
    )(aff_t.reshape(n_exp, nc, LANES))


def _ffn_kernel(idx_ref, hp_hbm, aff_hbm, wg_ref, wu_ref, wd_ref, o_ref, hsel, gsel, hb, gate_sc, act, sem,
                *, n_exp, cap, nf, nn, tf, half):
    e = pl.program_id(0)
    s = pl.program_id(1)
    n_steps = nf + nn
    rows_per_step = cap // nn
    nxt = jnp.minimum(e + 1, n_exp - 1)

    def start_row(ee, r, lane):
        t = idx_ref[ee * cap + r]
        pltpu.make_async_copy(hp_hbm.at[pl.ds(t, 1), :], hsel.at[pl.ds(r, 1), :], sem.at[0]).start(priority=lane % 2)
        pltpu.make_async_copy(aff_hbm.at[pl.ds(t, 1), :], gsel.at[pl.ds(r, 1), :],
                              sem.at[1]).start(priority=(lane + 1) % 2)

    def wait_gather():
        pltpu.make_async_copy(hp_hbm.at[pl.ds(0, cap), :], hsel, sem.at[0]).wait()
        pltpu.make_async_copy(aff_hbm.at[pl.ds(0, cap), :], gsel, sem.at[1]).wait()

    def gather_slice():
        for r in range(rows_per_step):
            start_row(nxt, (s - nf) * rows_per_step + r, r)

    @pl.when((e == 0) & (s == 0))
    def _():
        def go(g, carry):
            for k in range(GATHER_UNROLL):
                start_row(0, g * GATHER_UNROLL + k, k)
            return carry
        lax.fori_loop(0, cap // GATHER_UNROLL, go, 0)

    @pl.when(s == 0)
    def _():
        wait_gather()
        u = hsel[...]
        hb[:, :half] = lax.bitcast_convert_type(u << 16, F32).astype(BF16)
        hb[:, half:] = lax.bitcast_convert_type(u & jnp.uint32(0xFFFF0000), F32).astype(BF16)
        g = gsel[...]
        mine = jnp.where(lax.broadcasted_iota(I32, g.shape, 1) == e, g, 0.0)
        gate_sc[...] = jnp.broadcast_to(jnp.sum(mine, axis=-1, keepdims=True), g.shape)

    @pl.when(s < nf)
    def _():
        x = hb[...]
        u = jnp.dot(x, wg_ref[0].astype(BF16), preferred_element_type=F32)
        v = jnp.dot(x, wu_ref[0].astype(BF16), preferred_element_type=F32)
        act[s] = (u * jax.nn.sigmoid(u) * v).astype(BF16)

    @pl.when(s >= nf)
    def _():
        gather_slice()
        o = jnp.dot(act[0], wd_ref[0, 0:tf, :].astype(BF16), preferred_element_type=F32)
        for f in range(1, nf):
            o += jnp.dot(act[f], wd_ref[0, f * tf:(f + 1) * tf, :].astype(BF16), preferred_element_type=F32)
        o_ref[...] = (o * jnp.tile(gate_sc[...], (1, o.shape[1] // LANES))).astype(o_ref.dtype)

    @pl.when((e == n_exp - 1) & (s == n_steps - 1))
    def _():
        wait_gather()


def _expert_ffn(idx_flat, hp, aff_pad, w_gate, w_up, w_down):
    n_exp, d, ff = w_gate.shape
    cap = idx_flat.shape[0] // n_exp
    half = hp.shape[1]
    tf = _pick(ff, (256, 128))
    tn = _pick(d, (512, 256, 128))
    nf, nn = ff // tf, d // tn
    assert cap % nn == 0
    last_e = n_exp - 1

    def up_map(e, s, idx):
        return (jnp.where(s < nf, e, jnp.minimum(e + 1, last_e)), 0, jnp.where(s < nf, s, 0))

    def down_map(e, s, idx):
        return (e, 0, jnp.maximum(s - nf, 0))

    def out_map(e, s, idx):
        return (e, jnp.maximum(s - nf, 0))

    grid_spec = pltpu.PrefetchScalarGridSpec(
        num_scalar_prefetch=1,
        grid=(n_exp, nf + nn),
        in_specs=[pl.BlockSpec(memory_space=pl.ANY),
                  pl.BlockSpec(memory_space=pl.ANY),
                  pl.BlockSpec((1, d, tf), up_map),
                  pl.BlockSpec((1, d, tf), up_map),
                  pl.BlockSpec((1, ff, tn), down_map)],
        out_specs=pl.BlockSpec((cap, tn), out_map),
        scratch_shapes=[pltpu.VMEM((cap, half), U32), pltpu.VMEM((cap, LANES), F32),
                        pltpu.VMEM((cap, d), BF16), pltpu.VMEM((cap, LANES), F32),
                        pltpu.VMEM((nf, cap, tf), BF16), pltpu.SemaphoreType.DMA((2,))],
    )
    return pl.pallas_call(
        functools.partial(_ffn_kernel, n_exp=n_exp, cap=cap, nf=nf, nn=nn, tf=tf, half=half),
        grid_spec=grid_spec,
        out_shape=jax.ShapeDtypeStruct((n_exp * cap, d), BF16),
        compiler_params=_params(2),
        name="expert_ffn",
    )(idx_flat, hp, aff_pad, w_gate, w_up, w_down)


def _combine_kernel(off_ref, tot_ref, o_hbm, idx_hbm, x1_ref, g2_ref, lg_ref, lb_ref, out_ref, rbuf, ibuf, acc, sem,
                    *, tb, kbuf, max_pieces, n_blocks):
    b = pl.program_id(0)
    slot = b % 2
    pieces_per_round = kbuf // PIECE

    @pl.when(b == 0)
    def _():
        rbuf[...] = jnp.zeros_like(rbuf)
        ibuf[...] = jnp.full_like(ibuf, -1)

    def copies(blk, sl, j_global, j_local):
        src = pl.multiple_of(off_ref[blk * max_pieces + j_global], PIECE)
        dst = pl.multiple_of(j_local * PIECE, PIECE)
        return (pltpu.make_async_copy(o_hbm.at[pl.ds(src, PIECE), :], rbuf.at[sl, pl.ds(dst, PIECE), :],
                                      sem.at[0, sl]),
                pltpu.make_async_copy(idx_hbm.at[pl.ds(src, PIECE), :], ibuf.at[sl, pl.ds(dst, PIECE), :],
                                      sem.at[1, sl]))

    def pieces_in_round(blk, r):
        return jnp.clip(tot_ref[blk] - r * pieces_per_round, 0, pieces_per_round)

    def start_round(blk, sl, r):
        def go(j, c):
            for cp in copies(blk, sl, r * pieces_per_round + j, j):
                cp.start()
            return c
        lax.fori_loop(0, pieces_in_round(blk, r), go, 0)

    def wait_round(blk, sl, r):
        def go(j, c):
            for cp in copies(blk, sl, r * pieces_per_round + j, j):
                cp.wait()
            return c
        lax.fori_loop(0, pieces_in_round(blk, r), go, 0)

    @pl.when(b == 0)
    def _():
        start_round(0, 0, 0)

    @pl.when(b + 1 < n_blocks)
    def _():
        start_round(b + 1, 1 - slot, 0)

    acc[...] = jnp.zeros_like(acc)
    tok = b * tb + lax.broadcasted_iota(I32, (COMBINE_GROUP, tb), 1)

    def one_round(r, carry):
        @pl.when(r > 0)
        def _():
            start_round(b, slot, r)

        wait_round(b, slot, r)
        rows = pieces_in_round(b, r) * PIECE

        def group(g, c):
            r0 = pl.multiple_of(g * COMBINE_GROUP, COMBINE_GROUP)
            k = r0 + lax.broadcasted_iota(I32, (COMBINE_GROUP, tb), 0)
            ids = jnp.tile(ibuf[slot, pl.ds(r0, COMBINE_GROUP), :], (1, tb // LANES))
            onehot = jnp.where((ids == tok) & (k < rows), 1.0, 0.0).astype(BF16)
            acc[...] += lax.dot_general(onehot, rbuf[slot, pl.ds(r0, COMBINE_GROUP), :], (((0,), (0,)), ((), ())),
                                        preferred_element_type=F32)
            return c

        lax.fori_loop(0, (rows + COMBINE_GROUP - 1) // COMBINE_GROUP, group, 0)
        return carry

    lax.fori_loop(0, (tot_ref[b] + pieces_per_round - 1) // pieces_per_round, one_round, 0)
    z = DEEPNORM_ALPHA * x1_ref[...] + g2_ref[0:1, :] * acc[...]
    out_ref[...] = _ln_rows(z, LN_EPS) * lg_ref[...] + lb_ref[...]


def _combine(expert_out, idx, x1, mods, ln_g, ln_b):
    n_exp, cap = idx.shape
    s_len, d = x1.shape
    tb = _pick(s_len, (256, 128))
    nb = s_len // tb
    max_pieces = n_exp * (cap // PIECE)
    kbuf = min(COMBINE_ROWS, max_pieces * PIECE)

    edges = jnp.arange(nb + 1, dtype=I32) * tb
    below = jnp.sum((idx[None, :, :] < edges[:, None, None]).astype(I32), axis=2)
    s0, s1 = below[:-1], below[1:]
    p0 = s0 // PIECE
    n_pc = jnp.where(s1 > s0, (s1 + PIECE - 1) // PIECE - p0, 0)
    cum = jnp.cumsum(n_pc, axis=1)
    total = cum[:, -1]
    j = jnp.arange(max_pieces, dtype=I32)[None, :]
    e_of = jnp.minimum(jnp.sum((j[:, :, None] >= cum[:, None, :]).astype(I32), axis=2), n_exp - 1)
    mine = e_of[:, :, None] == jnp.arange(n_exp, dtype=I32)[None, None, :]
    first_piece = jnp.sum(jnp.where(mine, (p0 - (cum - n_pc))[:, None, :], 0), axis=2)
    off = jnp.where(j < total[:, None], e_of * cap + (first_piece + j) * PIECE, 0).astype(I32)

    idx_rep = jnp.broadcast_to(idx.reshape(n_exp * cap, 1), (n_exp * cap, LANES))
    grid_spec = pltpu.PrefetchScalarGridSpec(
        num_scalar_prefetch=2,
        grid=(nb,),
        in_specs=[pl.BlockSpec(memory_space=pl.ANY),
                  pl.BlockSpec(memory_space=pl.ANY),
                  pl.BlockSpec((tb, d), lambda i, off, tot: (i, 0)),
                  pl.BlockSpec((SUBLANES, d), lambda i, off, tot: (0, 5)),
                  pl.BlockSpec((1, d), lambda i, off, tot: (0, 0)),
                  pl.BlockSpec((1, d), lambda i, off, tot: (0, 0))],
        out_specs=pl.BlockSpec((tb, d), lambda i, off, tot: (i, 0)),
        scratch_shapes=[pltpu.VMEM((2, kbuf, d), BF16), pltpu.VMEM((2, kbuf, LANES), I32),
                        pltpu.VMEM((tb, d), F32), pltpu.SemaphoreType.DMA((2, 2))],
    )
    return pl.pallas_call(
        functools.partial(_combine_kernel, tb=tb, kbuf=kbuf, max_pieces=max_pieces, n_blocks=nb),
        grid_spec=grid_spec,
        out_shape=jax.ShapeDtypeStruct((s_len, d), F32),
        compiler_params=_params(1),
        name="combine_ln2",
    )(off.reshape(-1), total.astype(I32), expert_out, idx_rep, x1, mods, ln_g.reshape(1, d), ln_b.reshape(1, d))


def kernel(x, c, ctx, c_ctx, w_mod, b_mod, w_in, diff_lambda, diff_subln_g, w_pool, pool_scale, w_branch_attn,
           w_branch_pool, w_out, ln1_g, ln1_b, w_router, w_gate, w_up, w_down, ln2_g, ln2_b):
    batch, s_len, d = x.shape
    assert batch == 1 and w_mod.shape[0] == DEPTH
    attn_w = N_HEADS * V_DIM
    pool_w = w_pool.shape[1] * w_pool.shape[2]
    k_off, v_off, p_off = attn_w, 2 * attn_w, 3 * attn_w
    g_off = p_off + pool_w
    n_exp = w_router.shape[-1]
    cap = EC_CAPACITY_FACTOR * s_len // n_exp

    mods = _modulation(c, c_ctx, w_mod[0], b_mod[0])
    h = _premod(x[0], mods, 0)
    hc = _premod(ctx[0], mods, 1)

    w = w_in[0]
    tabs = _rope_tables(s_len)
    q = _project(h, w, 0, attn_w, BF16, tabs, HEAD_DIM ** -0.5 * math.log2(math.e))
    k = _project(h, w, k_off, attn_w, BF16, tabs)
    v = _project(h, w, v_off, attn_w, BF16)
    kc = _project(hc, w, k_off, attn_w, BF16)
    vc = _project(hc, w, v_off, attn_w, BF16)
    p = _project(h, w, p_off, pool_w, F32)
    gates = _project(h, w, g_off, 2 * d, F32)

    attn = _attention(q, k, v, kc, vc, diff_lambda[0], diff_subln_g[0])
    pooled = _pool(p, w_pool[0], pool_scale[0])
    merged = _merge(attn, pooled, w_branch_attn[0], w_branch_pool[0], gates)
    y = _project(merged, w_out[0], 0, d, F32)

    x1, hp, aff_t = _post1(x[0], y, mods, ln1_g[0], ln1_b[0], w_router[0].T)
    idx = _route(aff_t, cap).reshape(n_exp, cap)
    aff_pad = jnp.pad(aff_t.T, ((0, 0), (0, LANES - n_exp)))
    expert_out = _expert_ffn(idx.reshape(-1), hp, aff_pad, w_gate[0], w_up[0], w_down[0])
    out = _combine(expert_out, idx, x1, mods, ln2_g[0], ln2_b[0])
    return out.reshape(batch, s_len, d)
```
